```python
import jax, jax.numpy as jnp
from jax import lax
import numpy as np

D_MODEL = 1024
BATCH = 2
SEQ = 8192
DEPTH = 4
DEC_BATCH = 32
DEC_SEQ = 32
PAST_LEN = 4096

CHUNK = 64
Q_BLOCK = 128
FOX_HEADS = 8
FOX_HEAD_DIM = D_MODEL // 16
FOX_WIDTH = FOX_HEADS * FOX_HEAD_DIM
GLA_HEADS = 4
GLA_VALUE_WIDTH = D_MODEL - FOX_WIDTH
GLA_KEY_WIDTH = GLA_VALUE_WIDTH // 2
GLA_DK = GLA_KEY_WIDTH // GLA_HEADS
GLA_DV = GLA_VALUE_WIDTH // GLA_HEADS
GLA_LOW_RANK = 16
GLA_TAU = 16.0
MIX_WIDTH = FOX_WIDTH + GLA_VALUE_WIDTH
IN_SPLITS = (FOX_WIDTH, FOX_WIDTH, FOX_WIDTH, FOX_HEADS,
             GLA_KEY_WIDTH, GLA_KEY_WIDTH, GLA_VALUE_WIDTH, GLA_VALUE_WIDTH, GLA_LOW_RANK)
IN_WIDTH = sum(IN_SPLITS)
D_FF = 2816
N_EXPERTS = 8
TOP_K = 2
D_FF_EXPERT = 2816
N_DENSE = (DEPTH + 1) // 2
N_MOE = DEPTH // 2
FORGET_BIAS_CENTER = 3.0
EPS = 1e-6

kernel_name = 'fox_gla_adaln_streaming_encoder_step'


def rms_norm(x, g):
    xf = x.astype(jnp.float32)
    y = xf * lax.rsqrt(jnp.mean(jnp.square(xf), axis=-1, keepdims=True) + EPS)
    return (y * g.astype(jnp.float32)).astype(x.dtype)


def ada_modulation(c, w, b):
    m = jax.nn.silu(c) @ w + b
    return jnp.split(m[:, None, :], 6, axis=-1)


def modulate(x, g, shift, scale):
    return rms_norm(x, g) * (1 + scale) + shift


def mixer_inputs(h, w_in, b_f, w_a2, b_a):
    B, T, _ = h.shape
    z = h @ w_in
    offs, o = [], 0
    for n in IN_SPLITS[:-1]:
        o += n
        offs.append(o)
    fq, fk, fv, ff, gq, gk, gv, gr, ga = jnp.split(z, offs, axis=-1)
    fq = fq.reshape(B, T, FOX_HEADS, FOX_HEAD_DIM) * (FOX_HEAD_DIM ** -0.5)
    fk = fk.reshape(B, T, FOX_HEADS, FOX_HEAD_DIM)
    fv = fv.reshape(B, T, FOX_HEADS, FOX_HEAD_DIM)
    lf = jax.nn.log_sigmoid((ff + b_f).astype(jnp.float32))
    gq = gq.reshape(B, T, GLA_HEADS, GLA_DK) * (GLA_DK ** -0.5)
    gk = gk.reshape(B, T, GLA_HEADS, GLA_DK)
    gv = gv.reshape(B, T, GLA_HEADS, GLA_DV)
    la = jax.nn.log_sigmoid((ga @ w_a2 + b_a).astype(jnp.float32)) / GLA_TAU
    la = la.reshape(B, T, GLA_HEADS, GLA_DK)
    return fq, fk, fv, lf, gq, gk, gv, la, gr


def fox_block(q, F_q, pos_q, k, v, F_k, pos_k):
    s = jnp.einsum('bqhd,bkhd->bhqk', q, k).astype(jnp.float32)
    s = s + (jnp.swapaxes(F_q, 1, 2)[..., :, None] - jnp.swapaxes(F_k, 1, 2)[..., None, :])
    mask = pos_k[None, :] <= pos_q[:, None]
    s = jnp.where(mask, s, -jnp.inf)
    p = jax.nn.softmax(s, axis=-1).astype(v.dtype)
    return jnp.einsum('bhqk,bkhd->bqhd', p, v)


def fox_prompt(q, k, v, lf):
    B, S, H, Dh = q.shape
    F = jnp.cumsum(lf, axis=1)
    nb = S // Q_BLOCK
    pos = jnp.arange(S)
    qb = jnp.moveaxis(q.reshape(B, nb, Q_BLOCK, H, Dh), 1, 0)
    Fb = jnp.moveaxis(F.reshape(B, nb, Q_BLOCK, H), 1, 0)
    pb = pos.reshape(nb, Q_BLOCK)
    out = lax.map(lambda a: fox_block(a[0], a[1], a[2], k, v, F, pos), (qb, Fb, pb))
    return jnp.moveaxis(out, 0, 1).reshape(B, S, H, Dh)


def fox_sample(q, k_new, v_new, lf_new, k_cache, v_cache, lf_cache):
    P = k_cache.shape[1]
    T = q.shape[1]
    k = jnp.concatenate([k_cache, k_new], axis=1)
    v = jnp.concatenate([v_cache, v_new], axis=1)
    F = jnp.cumsum(jnp.concatenate([lf_cache.astype(jnp.float32), lf_new], axis=1), axis=1)
    pos = jnp.arange(P + T)
    return fox_block(q, F[:, P:], pos[P:], k, v, F, pos)


def gla_chunk(S0, q, k, v, la):
    qf, kf, vf = q.astype(jnp.float32), k.astype(jnp.float32), v.astype(jnp.float32)
    C = q.shape[1]
    b = jnp.cumsum(la, axis=1)
    causal = jnp.tril(jnp.ones((C, C), dtype=bool))
    diff = b[:, :, None] - b[:, None, :]
    decay = jnp.exp(jnp.where(causal[None, :, :, None, None], diff, -jnp.inf))
    A = jnp.einsum('bihd,bjhd,bijhd->bhij', qf, kf, decay)
    o_intra = jnp.einsum('bhij,bjhe->bihe', A, vf)
    o_inter = jnp.einsum('bihd,bhde->bihe', qf * jnp.exp(b), S0)
    b_last = b[:, -1]
    k_dec = kf * jnp.exp(b_last[:, None] - b)
    S1 = jnp.exp(b_last)[..., None] * S0 + jnp.einsum('bjhd,bjhe->bhde', k_dec, vf)
    return S1, o_intra + o_inter


def gla_prompt(q, k, v, la):
    B, S = q.shape[:2]
    n = S // CHUNK

    def to_chunks(a):
        return jnp.moveaxis(a.reshape((B, n, CHUNK) + a.shape[2:]), 1, 0)

    S0 = jnp.zeros((B, GLA_HEADS, GLA_DK, GLA_DV), jnp.float32)
    S_fin, o = lax.scan(lambda st, inp: gla_chunk(st, *inp), S0,
                        (to_chunks(q), to_chunks(k), to_chunks(v), to_chunks(la)))
    return jnp.moveaxis(o, 0, 1).reshape(B, S, GLA_HEADS, GLA_DV), S_fin


def mixer_output(fox_o, gla_o, gr, g_gla, w_o):
    B, T = fox_o.shape[:2]
    gla_n = rms_norm(gla_o, g_gla).astype(gr.dtype).reshape(B, T, GLA_VALUE_WIDTH) * jax.nn.silu(gr)
    cat = jnp.concatenate([fox_o.reshape(B, T, FOX_WIDTH).astype(gr.dtype), gla_n], axis=-1)
    return cat @ w_o


def swiglu(h, wg, wu, wd):
    return (jax.nn.silu(h @ wg) * (h @ wu)) @ wd


def moe_swiglu(h, w_router, wg, wu, wd):
    logits = (h @ w_router).astype(jnp.float32)
    top_v, top_i = lax.top_k(logits, TOP_K)
    gates = jax.nn.softmax(top_v, axis=-1)
    dense_gate = jnp.sum(jax.nn.one_hot(top_i, N_EXPERTS, dtype=jnp.float32) * gates[..., None], axis=-2)
    dense_gate = dense_gate.astype(h.dtype)
    out = jnp.zeros_like(h)
    for e in range(N_EXPERTS):
        out = out + dense_gate[..., e:e + 1] * swiglu(h, wg[e], wu[e], wd[e])
    return out


def channel_mixer(l, h, w_ffn_gate, w_ffn_up, w_ffn_down, w_router, w_moe_gate, w_moe_up, w_moe_down):
    i = l // 2
    if l % 2 == 0:
        return swiglu(h, w_ffn_gate[i], w_ffn_up[i], w_ffn_down[i])
    return moe_swiglu(h, w_router[i], w_moe_gate[i], w_moe_up[i], w_moe_down[i])


def setup_inputs(seed: int = 0) -> dict:
    key = jax.random.key(seed)
    ks = iter(jax.random.split(key, 32))
    D = D_MODEL

    def nrm(shape, s):
        return jax.random.normal(next(ks), shape, jnp.float32) * s

    return {
        'x_prompt': nrm((BATCH, SEQ, D), 1.0),
        'x_sample': nrm((DEC_BATCH, DEC_SEQ, D), 1.0),
        'cache_fox_k': nrm((DEPTH, DEC_BATCH, PAST_LEN, FOX_HEADS, FOX_HEAD_DIM), 1.0),
        'cache_fox_v': nrm((DEPTH, DEC_BATCH, PAST_LEN, FOX_HEADS, FOX_HEAD_DIM), 1.0),
        'cache_fox_lf': jax.nn.log_sigmoid(FORGET_BIAS_CENTER + nrm((DEPTH, DEC_BATCH, PAST_LEN, FOX_HEADS), 1.0)),
        'state_gla': nrm((DEPTH, DEC_BATCH, GLA_HEADS, GLA_DK, GLA_DV), 0.5),
        'c_prompt': nrm((BATCH, D), 1.0),
        'c_sample': nrm((DEC_BATCH, D), 1.0),
        'w_ada': nrm((DEPTH, D, 6 * D), 0.5 * D ** -0.5),
        'b_ada': nrm((DEPTH, 6 * D), 0.02),
        'g_attn': 1.0 + nrm((DEPTH, D), 0.02),
        'g_ffn': 1.0 + nrm((DEPTH, D), 0.02),
        'w_in': nrm((DEPTH, D, IN_WIDTH), D ** -0.5),
        'b_fox_f': FORGET_BIAS_CENTER + nrm((DEPTH, FOX_HEADS), 0.5),
        'w_gla_a2': nrm((DEPTH, GLA_LOW_RANK, GLA_KEY_WIDTH), GLA_LOW_RANK ** -0.5),
        'b_gla_a': nrm((DEPTH, GLA_KEY_WIDTH), 0.02),
        'g_gla': 1.0 + nrm((DEPTH, GLA_DV), 0.02),
        'w_o': nrm((DEPTH, MIX_WIDTH, D), MIX_WIDTH ** -0.5),
        'w_ffn_gate': nrm((N_DENSE, D, D_FF), D ** -0.5),
        'w_ffn_up': nrm((N_DENSE, D, D_FF), D ** -0.5),
        'w_ffn_down': nrm((N_DENSE, D_FF, D), D_FF ** -0.5),
        'w_router': nrm((N_MOE, D, N_EXPERTS), D ** -0.5),
        'w_moe_gate': nrm((N_MOE, N_EXPERTS, D, D_FF_EXPERT), D ** -0.5),
        'w_moe_up': nrm((N_MOE, N_EXPERTS, D, D_FF_EXPERT), D ** -0.5),
        'w_moe_down': nrm((N_MOE, N_EXPERTS, D_FF_EXPERT, D), D_FF_EXPERT ** -0.5),
        'g_final': 1.0 + nrm((D,), 0.02),
    }


def reference(x_prompt, x_sample, cache_fox_k, cache_fox_v, cache_fox_lf, state_gla,
              c_prompt, c_sample, w_ada, b_ada, g_attn, g_ffn, w_in, b_fox_f, w_gla_a2,
              b_gla_a, g_gla, w_o, w_ffn_gate, w_ffn_up, w_ffn_down, w_router, w_moe_gate,
              w_moe_up, w_moe_down, g_final):
    xp, xs = x_prompt, x_sample
    kp, vp, lp, sp = [], [], [], []
    kn, vn, ln, sn = [], [], [], []
    for l in range(DEPTH):
        mp = ada_modulation(c_prompt, w_ada[l], b_ada[l])
        ms = ada_modulation(c_sample, w_ada[l], b_ada[l])

        hp = modulate(xp, g_attn[l], mp[0], mp[1])
        fq, fk, fv, lf, gq, gk, gv, la, gr = mixer_inputs(hp, w_in[l], b_fox_f[l], w_gla_a2[l], b_gla_a[l])
        fo = fox_prompt(fq, fk, fv, lf)
        go, gst = gla_prompt(gq, gk, gv, la)
        xp = xp + mp[2] * mixer_output(fo, go, gr, g_gla[l], w_o[l])
        kp.append(fk); vp.append(fv); lp.append(lf); sp.append(gst)

        hs = modulate(xs, g_attn[l], ms[0], ms[1])
        fq, fk, fv, lf, gq, gk, gv, la, gr = mixer_inputs(hs, w_in[l], b_fox_f[l], w_gla_a2[l], b_gla_a[l])
        fo = fox_sample(fq, fk, fv, lf, cache_fox_k[l], cache_fox_v[l], cache_fox_lf[l])
        gst, go = gla_chunk(state_gla[l].astype(jnp.float32), gq, gk, gv, la)
        xs = xs + ms[2] * mixer_output(fo, go, gr, g_gla[l], w_o[l])
        kn.append(fk); vn.append(fv); ln.append(lf); sn.append(gst)

        xp = xp + mp[5] * channel_mixer(l, modulate(xp, g_ffn[l], mp[3], mp[4]), w_ffn_gate, w_ffn_up,
                                        w_ffn_down, w_router, w_moe_gate, w_moe_up, w_moe_down)
        xs = xs + ms[5] * channel_mixer(l, modulate(xs, g_ffn[l], ms[3], ms[4]), w_ffn_gate, w_ffn_up,
                                        w_ffn_down, w_router, w_moe_gate, w_moe_up, w_moe_down)

    y_prompt = rms_norm(xp, g_final)
    y_sample = rms_norm(xs, g_final)
    return (y_prompt, y_sample,
            jnp.stack(kp), jnp.stack(vp), jnp.stack(lp), jnp.stack(sp),
            jnp.stack(kn), jnp.stack(vn), jnp.stack(ln), jnp.stack(sn))
```

```python
import functools

import jax
import jax.numpy as jnp
from jax import lax
from jax.experimental import pallas as pl
from jax.experimental.pallas import tpu as pltpu

F32, BF16, I32 = jnp.float32, jnp.bfloat16, jnp.int32
EPS = 1e-6
NEG = -1e30
GROUP = 32
FOX_HEADS, FOX_DH = 8, 64
GLA_HEADS, GLA_DK, GLA_DV = 4, 64, 128
GLA_RANK = 16
GLA_TAU = 16.0
N_EXPERTS = 8
SUB = 16
VMEM_LIMIT = 56 * 1024 * 1024


def _cp(*sem):
    return pltpu.CompilerParams(dimension_semantics=sem, vmem_limit_bytes=VMEM_LIMIT)


def _dot(a, b):
    return jnp.dot(a, b, preferred_element_type=F32)


def _dot_nt(a, b):
    return lax.dot_general(a, b, (((1,), (1,)), ((), ())), preferred_element_type=F32)


def _dot_tn(a, b):
    return lax.dot_general(a, b, (((0,), (0,)), ((), ())), preferred_element_type=F32)


def _split3(x):
    h = x.astype(BF16)
    r = x - h.astype(F32)
    m = r.astype(BF16)
    l = (r - m.astype(F32)).astype(BF16)
    return h, m, l


def _dot3(x, u):
    h, m, l = _split3(x)
    return _dot(h, u) + _dot(m, u) + _dot(l, u)


def _dot3_l(u, x):
    h, m, l = _split3(x)
    return _dot(u, h) + _dot(u, m) + _dot(u, l)


def _silu(x):
    return x * (1.0 / (1.0 + jnp.exp(-x)))


def _log_sigmoid(x):
    return jnp.minimum(x, 0.0) - jnp.log1p(jnp.exp(-jnp.abs(x)))


def _norm_mod(x3, g, shift, scale):
    ms = jnp.mean(x3 * x3, axis=-1, keepdims=True)
    return (x3 * lax.rsqrt(ms + EPS) * g) * (1.0 + scale) + shift


def _iota(shape, axis):
    return lax.broadcasted_iota(I32, shape, axis)


def _ones_where(cond):
    return jnp.where(cond, 1.0, 0.0).astype(BF16)


def _ada_kernel(c_ref, w_ref, b_ref, o_ref):
    a = _silu(c_ref[...]).astype(BF16)
    o_ref[0] = _dot(a, w_ref[0].astype(BF16)) + b_ref[0]


def _ada(c_all, w_ada, b_ada):
    L, D, N = w_ada.shape
    R = c_all.shape[0]
    tn = 1536
    return pl.pallas_call(
        _ada_kernel,
        grid=(L, N // tn),
        in_specs=[pl.BlockSpec((R, D), lambda l, j: (0, 0)),
                  pl.BlockSpec((1, D, tn), lambda l, j: (l, 0, j)),
                  pl.BlockSpec((1, 1, tn), lambda l, j: (l, 0, j))],
        out_specs=pl.BlockSpec((1, R, tn), lambda l, j: (l, 0, j)),
        out_shape=jax.ShapeDtypeStruct((L, R, N), F32),
        compiler_params=_cp("parallel", "parallel"),
    )(c_all, w_ada, b_ada.reshape(L, 1, N))


def _inproj_kernel(x_ref, sh_ref, sc_ref, g_ref, w_ref, wm_ref, wf_ref, bf_ref, wa_ref, ba_ref,
                   fq_ref, fk32_ref, fv32_ref, fkb_ref, fvb_ref, gq_ref, gk_ref, gv_ref, gr_ref,
                   la_ref, lft_ref, ft_ref, carry_ref, *, blocks_per_seq):
    i = pl.program_id(0)
    G, R, D = x_ref.shape
    tm = G * R
    h = _norm_mod(x_ref[...], g_ref[...], sh_ref[...], sc_ref[...]).reshape(tm, D).astype(BF16)

    fq_ref[...] = (_dot(h, w_ref[:, 0:512]) * (FOX_DH ** -0.5)).astype(BF16)
    z = _dot(h, w_ref[:, 512:1024])
    fk32_ref[...] = z
    fkb_ref[...] = z.astype(BF16)
    z = _dot(h, w_ref[:, 1024:1536])
    fv32_ref[...] = z
    fvb_ref[...] = z.astype(BF16)
    gq_ref[...] = _dot(h, w_ref[:, 1536:1792]) * (GLA_DK ** -0.5)
    gk_ref[...] = _dot(h, w_ref[:, 1792:2048])
    gv_ref[...] = _dot(h, w_ref[:, 2048:2560]).astype(BF16)
    gr_ref[...] = _dot(h, w_ref[:, 2560:3072])

    zm = _dot(h, wm_ref[...])
    la_pre = _dot(zm.astype(BF16), wa_ref[...]) + ba_ref[...]
    la_ref[...] = _log_sigmoid(la_pre) * (1.0 / GLA_TAU)

    lft = _log_sigmoid(_dot_nt(wf_ref[...], h) + bf_ref[...])
    lft_ref[...] = lft
    u = _ones_where(_iota((tm, tm), 0) <= _iota((tm, tm), 1))

    @pl.when(i % blocks_per_seq == 0)
    def _():
        carry_ref[...] = jnp.zeros_like(carry_ref)

    f = _dot3(lft, u) + carry_ref[:, 0:1]
    ft_ref[...] = f
    carry_ref[...] = jnp.broadcast_to(f[:, tm - 1:tm], carry_ref.shape)


def _inproj(x3, shift, scale, g, w, wm, wf, bf, wa, ba, *, tm, blocks_per_seq):
    NG, R, D = x3.shape
    T = NG * R
    G = tm // R
    nb = T // tm
    row = lambda n: pl.BlockSpec((tm, n), lambda i: (i, 0))
    full = lambda a: pl.BlockSpec(a.shape, lambda i: (0,) * a.ndim)
    outs = [
        (jax.ShapeDtypeStruct((T, 512), BF16), row(512)),
        (jax.ShapeDtypeStruct((T, 512), F32), row(512)),
        (jax.ShapeDtypeStruct((T, 512), F32), row(512)),
        (jax.ShapeDtypeStruct((T, 512), BF16), row(512)),
        (jax.ShapeDtypeStruct((T, 512), BF16), row(512)),
        (jax.ShapeDtypeStruct((T, 256), F32), row(256)),
        (jax.ShapeDtypeStruct((T, 256), F32), row(256)),
        (jax.ShapeDtypeStruct((T, 512), BF16), row(512)),
        (jax.ShapeDtypeStruct((T, 512), F32), row(512)),
        (jax.ShapeDtypeStruct((T, 256), F32), row(256)),
        (jax.ShapeDtypeStruct((16, T), F32), pl.BlockSpec((16, tm), lambda i: (0, i))),
        (jax.ShapeDtypeStruct((16, T), F32), pl.BlockSpec((16, tm), lambda i: (0, i))),
    ]
    return pl.pallas_call(
        functools.partial(_inproj_kernel, blocks_per_seq=blocks_per_seq),
        grid=(nb,),
        in_specs=[pl.BlockSpec((G, R, D), lambda i: (i, 0, 0)),
                  pl.BlockSpec((G, 1, D), lambda i: (i, 0, 0)),
                  pl.BlockSpec((G, 1, D), lambda i: (i, 0, 0)),
                  full(g), full(w), full(wm), full(wf), full(bf), full(wa), full(ba)],
        out_specs=[o[1] for o in outs],
        out_shape=[o[0] for o in outs],
        scratch_shapes=[pltpu.VMEM((16, 128), F32)],
        compiler_params=_cp("arbitrary"),
    )(x3, shift, scale, g, w, wm, wf, bf, wa, ba)


def _fox_prompt_kernel(q_ref, k_ref, v_ref, ft_ref, o_ref, m_scr, l_scr, acc_scr, *, tq, tk):
    i = pl.program_id(2)
    lo = _iota((1, 128), 1) < FOX_DH
    q2 = q_ref[...]
    zero = jnp.zeros_like(q2)
    qh = (jnp.where(lo, q2, zero), jnp.where(lo, zero, q2))
    q0 = pl.multiple_of(i * tq, tq)
    eye = _iota((tq, tq), 0) == _iota((tq, tq), 1)
    fq = []
    for hh in range(2):
        frow = ft_ref[0, 0, hh:hh + 1, pl.ds(q0, tq)]
        fq.append(jnp.sum(jnp.where(eye, frow, 0.0), axis=1, keepdims=True))
    m_scr[...] = jnp.full(m_scr.shape, NEG, F32)
    l_scr[...] = jnp.zeros_like(l_scr)
    acc_scr[...] = jnp.zeros_like(acc_scr)

    def step(kb, masked):
        k0 = pl.multiple_of(kb * tk, tk)
        k2 = k_ref[pl.ds(k0, tk), :]
        v2 = v_ref[pl.ds(k0, tk), :]
        pv, alphas = [], []
        for hh in range(2):
            s = _dot_nt(qh[hh], k2) + fq[hh] - ft_ref[0, 0, hh:hh + 1, pl.ds(k0, tk)]
            if masked:
                visible = (k0 + _iota((tq, tk), 1)) <= (q0 + _iota((tq, tk), 0))
                s = jnp.where(visible, s, NEG)
            m_old = m_scr[hh]
            m_new = jnp.maximum(m_old, jnp.max(s, axis=1, keepdims=True))
            alpha = jnp.exp(m_old - m_new)
            p = jnp.exp(s - m_new)
            l_scr[hh] = alpha * l_scr[hh] + jnp.sum(p, axis=1, keepdims=True)
            m_scr[hh] = m_new
            pv.append(_dot(p.astype(BF16), v2))
            alphas.append(alpha)
        acc_scr[...] = acc_scr[...] * jnp.where(lo, alphas[0], alphas[1]) + jnp.where(lo, pv[0], pv[1])

    n_full = (i * tq) // tk

    def body(kb, c):
        step(kb, False)
        return c

    lax.fori_loop(0, n_full, body, 0)
    for d in range(tq // tk):
        step(n_full + d, True)
    inv = jnp.where(lo, 1.0 / l_scr[0], 1.0 / l_scr[1])
    o_ref[...] = (acc_scr[...] * inv).astype(o_ref.dtype)


def _fox_prompt(fq, fkb, fvb, ft4, *, B, S, tq, tk):
    nq = S // tq
    return pl.pallas_call(
        functools.partial(_fox_prompt_kernel, tq=tq, tk=tk),
        grid=(B, FOX_HEADS // 2, nq),
        in_specs=[pl.BlockSpec((tq, 128), lambda b, p, i: (b * nq + i, p)),
                  pl.BlockSpec((S, 128), lambda b, p, i: (b, p)),
                  pl.BlockSpec((S, 128), lambda b, p, i: (b, p)),
                  pl.BlockSpec((1, 1, 2, S), lambda b, p, i: (b, p, 0, 0))],
        out_specs=pl.BlockSpec((tq, 128), lambda b, p, i: (b * nq + i, p)),
        out_shape=jax.ShapeDtypeStruct((B * S, 512), BF16),
        scratch_shapes=[pltpu.VMEM((2, tq, 1), F32), pltpu.VMEM((2, tq, 1), F32),
                        pltpu.VMEM((tq, 128), F32)],
        compiler_params=_cp("parallel", "parallel", "arbitrary"),
    )(fq, fkb, fvb, ft4)


def _fox_sample_kernel(q_ref, kn_ref, vn_ref, lfn_ref, kc_ref, vc_ref, lfc_ref, o_ref,
                       qbd_scr, fn_scr, m_scr, l_scr, acc_scr, g_scr, *, tk, nkb):
    kb = pl.program_id(1)
    TN = q_ref.shape[1]
    HT = FOX_HEADS * TN
    W = FOX_HEADS * FOX_DH

    def online(s, vblk):
        m_old = m_scr[...]
        m_new = jnp.maximum(m_old, jnp.max(s, axis=1, keepdims=True))
        alpha = jnp.exp(m_old - m_new)
        p = jnp.exp(s - m_new)
        l_scr[...] = alpha * l_scr[...] + jnp.sum(p, axis=1, keepdims=True)
        m_scr[...] = m_new
        acc_scr[...] = acc_scr[...] * alpha + _dot(p.astype(BF16), vblk)

    @pl.when(kb == 0)
    def _():
        q = q_ref[0]
        qt = jnp.concatenate([q] * FOX_HEADS, axis=0)
        same = (_iota((HT, W), 0) // TN) == (_iota((HT, W), 1) // FOX_DH)
        qbd = jnp.where(same, qt, jnp.zeros_like(qt))
        qbd_scr[...] = qbd
        tri = _ones_where(_iota((TN, TN), 0) <= _iota((TN, TN), 1))
        fnt = _dot3(lfn_ref[0], tri)
        rep = jnp.broadcast_to(fnt[:, None, :], (FOX_HEADS, TN, TN)).reshape(HT, TN)
        tpos = _iota((HT, TN), 0) % TN
        jpos = _iota((HT, TN), 1)
        fn_col = jnp.sum(jnp.where(tpos == jpos, rep, 0.0), axis=1, keepdims=True)
        fn_scr[...] = fn_col
        m_scr[...] = jnp.full(m_scr.shape, NEG, F32)
        l_scr[...] = jnp.zeros_like(l_scr)
        acc_scr[...] = jnp.zeros_like(acc_scr)
        g_scr[...] = jnp.zeros_like(g_scr)
        s = _dot_nt(qbd, kn_ref[0]) + fn_col - rep
        online(jnp.where(jpos <= tpos, s, NEG), vn_ref[0])

    lfc = lfc_ref[0]
    later = _ones_where(_iota((tk, tk), 0) > _iota((tk, tk), 1))
    gl = _dot3(lfc, later) + g_scr[:, 0:1]
    g_scr[...] = jnp.broadcast_to(gl[:, 0:1] + lfc[:, 0:1], g_scr.shape)
    grep = jnp.broadcast_to(gl[:, None, :], (FOX_HEADS, TN, tk)).reshape(HT, tk)
    s = _dot_nt(qbd_scr[...], kc_ref[0].astype(BF16)) + fn_scr[...] + grep
    online(s, vc_ref[0].astype(BF16))

    @pl.when(kb == nkb - 1)
    def _():
        accn = acc_scr[...] * (1.0 / l_scr[...])
        lane_head = _iota((TN, W), 1) // FOX_DH
        out = jnp.zeros((TN, W), F32)
        for h in range(FOX_HEADS):
            out = out + jnp.where(lane_head == h, accn[h * TN:(h + 1) * TN, :], 0.0)
        o_ref[...] = out.astype(o_ref.dtype)


def _fox_sample(q3, kn3, vn3, lfn3, kc, vc, lfc_t, *, tk):
    NB, TN, W = q3.shape
    P = kc.shape[1]
    nkb = P // tk
    HT = FOX_HEADS * TN
    new = lambda: pl.BlockSpec((1, TN, W), lambda b, k: (b, 0, 0))
    return pl.pallas_call(
        functools.partial(_fox_sample_kernel, tk=tk, nkb=nkb),
        grid=(NB, nkb),
        in_specs=[new(), new(), new(),
                  pl.BlockSpec((1, FOX_HEADS, TN), lambda b, k: (b, 0, 0)),
                  pl.BlockSpec((1, tk, W), lambda b, k: (b, nkb - 1 - k, 0)),
                  pl.BlockSpec((1, tk, W), lambda b, k: (b, nkb - 1 - k, 0)),
                  pl.BlockSpec((1, FOX_HEADS, tk), lambda b, k: (b, 0, nkb - 1 - k))],
        out_specs=pl.BlockSpec((TN, W), lambda b, k: (b, 0)),
        out_shape=jax.ShapeDtypeStruct((NB * TN, W), BF16),
        scratch_shapes=[pltpu.VMEM((HT, W), BF16), pltpu.VMEM((HT, 1), F32), pltpu.VMEM((HT, 1), F32),
                        pltpu.VMEM((HT, 1), F32), pltpu.VMEM((HT, W), F32), pltpu.VMEM((FOX_HEADS, 128), F32)],
        compiler_params=_cp("parallel", "arbitrary"),
    )(q3, kn3, vn3, lfn3, kc, vc, lfc_t)


def _gla_kernel(q_ref, k_ref, v_ref, la_ref, s0_ref, segb_ref, o_ref, sfin_ref, s_scr, *, C, nchunk):
    c = pl.program_id(1)
    KW = GLA_HEADS * GLA_DK

    @pl.when(c == 0)
    def _():
        s_scr[...] = s0_ref[0]

    q = q_ref[...]
    k = k_ref[...]
    la = la_ref[...]
    v = v_ref[...]
    tri = _ones_where(_iota((C, C), 1) <= _iota((C, C), 0))
    b = _dot3_l(tri, la)
    blast = b[C - 1:C, :]
    qe = (q * jnp.exp(b)).astype(BF16)
    kd = (k * jnp.exp(blast - b)).astype(BF16)
    s_old = s_scr[...]
    s_bf = s_old.astype(BF16)

    for blk in range(C // SUB):
        r0 = blk * SUB
        bi, qi, ki = b[r0:r0 + SUB], q[r0:r0 + SUB], k[r0:r0 + SUB]
        diff = bi[None, :, :] - bi[:, None, :]
        causal = _iota((SUB, SUB, 1), 1) >= _iota((SUB, SUB, 1), 0)
        tm3 = jnp.exp(jnp.where(causal, diff, -jnp.inf)) * qi[None, :, :] * ki[:, None, :]
        arep = _dot(tm3.reshape(SUB * SUB, KW).astype(BF16), segb_ref[...])
        vi = v[r0:r0 + SUB].astype(F32)
        o_blk = jnp.sum(arep.reshape(SUB, SUB, GLA_HEADS * GLA_DV) * vi[:, None, :], axis=0)
        parts = []
        if blk > 0:
            bref = b[r0 - 1:r0]
            qs = (qi * jnp.exp(bi - bref)).astype(BF16)
            ks = (k[:r0] * jnp.exp(bref - b[:r0])).astype(BF16)
        for h in range(GLA_HEADS):
            ks_, vs_ = slice(h * GLA_DK, (h + 1) * GLA_DK), slice(h * GLA_DV, (h + 1) * GLA_DV)
            o_h = _dot(qe[r0:r0 + SUB, ks_], s_bf[ks_, :])
            if blk > 0:
                a = _dot_nt(qs[:, ks_], ks[:, ks_])
                o_h = o_h + _dot(a.astype(BF16), v[:r0, vs_])
            parts.append(o_h)
        o_ref[r0:r0 + SUB, :] = o_blk + jnp.concatenate(parts, axis=1)

    dec = jnp.exp(_dot3_tn(la, jnp.ones((C, GLA_DV), BF16)))
    upd = _dot_tn(kd, v)
    upd = jnp.concatenate([upd[h * GLA_DK:(h + 1) * GLA_DK, h * GLA_DV:(h + 1) * GLA_DV]
                           for h in range(GLA_HEADS)], axis=0)
    s_new = dec * s_old + upd
    s_scr[...] = s_new

    @pl.when(c == nchunk - 1)
    def _():
        sfin_ref[0] = s_new


def _dot3_tn(x, u):
    h, m, l = _split3(x)
    return _dot_tn(h, u) + _dot_tn(m, u) + _dot_tn(l, u)


def _gla(gq, gk, gv, la, s0, segb, *, C, nseq, nchunk, row_block_off):
    KW, VW = GLA_HEADS * GLA_DK, GLA_HEADS * GLA_DV
    tok = lambda n: pl.BlockSpec((C, n), lambda s, c: (row_block_off + s * nchunk + c, 0))
    return pl.pallas_call(
        functools.partial(_gla_kernel, C=C, nchunk=nchunk),
        grid=(nseq, nchunk),
        in_specs=[tok(KW), tok(KW), tok(VW), tok(KW),
                  pl.BlockSpec((1, KW, GLA_DV), lambda s, c: (s, 0, 0)),
                  pl.BlockSpec(segb.shape, lambda s, c: (0, 0))],
        out_specs=[pl.BlockSpec((C, VW), lambda s, c: (s * nchunk + c, 0)),
                   pl.BlockSpec((1, KW, GLA_DV), lambda s, c: (s, 0, 0))],
        out_shape=[jax.ShapeDtypeStruct((nseq * nchunk * C, VW), F32),
                   jax.ShapeDtypeStruct((nseq, KW, GLA_DV), F32)],
        scratch_shapes=[pltpu.VMEM((KW, GLA_DV), F32)],
        compiler_params=_cp("parallel", "arbitrary"),
    )(gq, gk, gv, la, s0, segb)


def _outproj_kernel(fop_ref, fos_ref, gop_ref, gos_ref, gr_ref, x_ref, gate_ref, gg_ref, wo_ref, o_ref, *, nbp):
    G, R, D = x_ref.shape
    is_prompt = pl.program_id(0) < nbp
    fo = jnp.where(is_prompt, fop_ref[...], fos_ref[...])
    go = jnp.where(is_prompt, gop_ref[...], gos_ref[...])
    parts = []
    for h in range(GLA_HEADS):
        seg = go[:, h * GLA_DV:(h + 1) * GLA_DV]
        ms = jnp.mean(seg * seg, axis=-1, keepdims=True)
        parts.append(seg * lax.rsqrt(ms + EPS) * gg_ref[...])
    gn = jnp.concatenate(parts, axis=1) * _silu(gr_ref[...])
    mix = _dot(fo, wo_ref[0:512, :]) + _dot(gn.astype(BF16), wo_ref[512:1024, :])
    o_ref[...] = x_ref[...] + gate_ref[...] * mix.reshape(G, R, D)


def _outproj(fo_p, fo_s, go_p, go_s, gr, x3, gate, gg, wo, *, tm):
    NG, R, D = x3.shape
    G = tm // R
    nbp = fo_p.shape[0] // tm
    row = lambda n: pl.BlockSpec((tm, n), lambda i: (i, 0))
    prow = lambda n: pl.BlockSpec((tm, n), lambda i: (jnp.minimum(i, nbp - 1), 0))
    srow = lambda n: pl.BlockSpec((tm, n), lambda i: (jnp.maximum(i - nbp, 0), 0))
    return pl.pallas_call(
        functools.partial(_outproj_kernel, nbp=nbp),
        grid=(NG // G,),
        in_specs=[prow(512), srow(512), prow(512), srow(512), row(512),
                  pl.BlockSpec((G, R, D), lambda i: (i, 0, 0)),
                  pl.BlockSpec((G, 1, D), lambda i: (i, 0, 0)),
                  pl.BlockSpec(gg.shape, lambda i: (0, 0)),
                  pl.BlockSpec(wo.shape, lambda i: (0, 0))],
        out_specs=pl.BlockSpec((G, R, D), lambda i: (i, 0, 0)),
        out_shape=jax.ShapeDtypeStruct(x3.shape, F32),
        compiler_params=_cp("parallel"),
    )(fo_p, fo_s, go_p, go_s, gr, x3, gate, gg, wo)


def _ffn_kernel(x_ref, sh_ref, sc_ref, gate_ref, g_ref, wg_ref, wu_ref, wd_ref, o_ref, h_scr, acc_scr, *, nf):
    j = pl.program_id(1)
    G, R, D = x_ref.shape

    @pl.when(j == 0)
    def _():
        h_scr[...] = _norm_mod(x_ref[...], g_ref[...], sh_ref[...], sc_ref[...]).reshape(G * R, D).astype(BF16)
        acc_scr[...] = jnp.zeros_like(acc_scr)

    h = h_scr[...]
    a = _dot(h, wg_ref[0].astype(BF16))
    u = _dot(h, wu_ref[0].astype(BF16))
    acc_scr[...] += _dot((_silu(a) * u).astype(BF16), wd_ref[0].astype(BF16))

    @pl.when(j == nf - 1)
    def _():
        o_ref[...] = x_ref[...] + gate_ref[...] * acc_scr[...].reshape(G, R, D)


def _ffn(x3, shift, scale, gate, g, wg, wu, wd, li, *, tm, tf):
    NG, R, D = x3.shape
    G = tm // R
    F = wg.shape[-1]
    nf = F // tf
    grp = lambda n: pl.BlockSpec((G, n, D), lambda i, j: (i, 0, 0))
    return pl.pallas_call(
        functools.partial(_ffn_kernel, nf=nf),
        grid=(NG // G, nf),
        in_specs=[grp(R), grp(1), grp(1), grp(1),
                  pl.BlockSpec(g.shape, lambda i, j: (0, 0)),
                  pl.BlockSpec((1, D, tf), lambda i, j: (li, 0, j)),
                  pl.BlockSpec((1, D, tf), lambda i, j: (li, 0, j)),
                  pl.BlockSpec((1, tf, D), lambda i, j: (li, j, 0))],
        out_specs=grp(R),
        out_shape=jax.ShapeDtypeStruct(x3.shape, F32),
        scratch_shapes=[pltpu.VMEM((tm, D), BF16), pltpu.VMEM((tm, D), F32)],
        compiler_params=_cp("parallel", "arbitrary"),
    )(x3, shift, scale, gate, g, wg, wu, wd)


def _router_kernel(x_ref, sh_ref, sc_ref, g_ref, wr_ref, h_ref, idx_ref, gw_ref):
    G, R, D = x_ref.shape
    h = _norm_mod(x_ref[...], g_ref[...], sh_ref[...], sc_ref[...]).reshape(G * R, D)
    h_ref[...] = h
    hh, hm, hl = _split3(h)
    wh, wm, wl = wr_ref[0], wr_ref[1], wr_ref[2]
    logits = (_dot(hh, wh) + _dot(hh, wm) + _dot(hm, wh)
              + _dot(hh, wl) + _dot(hm, wm) + _dot(hl, wh))
    lane = _iota(logits.shape, 1)
    logits = jnp.where(lane < N_EXPERTS, logits, -jnp.inf)
    v1 = jnp.max(logits, axis=1, keepdims=True)
    i1 = jnp.min(jnp.where(logits == v1, lane, 128), axis=1, keepdims=True)
    rest = jnp.where(lane == i1, -jnp.inf, logits)
    v2 = jnp.max(rest, axis=1, keepdims=True)
    i2 = jnp.min(jnp.where(rest == v2, lane, 128), axis=1, keepdims=True)
    e = jnp.exp(v2 - v1)
    g1 = 1.0 / (1.0 + e)
    idx_ref[...] = jnp.where(lane == 0, i1, jnp.where(lane == 1, i2, 0))
    gw_ref[...] = jnp.where(lane == 0, g1, jnp.where(lane == 1, e * g1, 0.0))


def _router(x3, shift, scale, g, wr3, *, tm):
    NG, R, D = x3.shape
    T = NG * R
    G = tm // R
    grp = lambda n: pl.BlockSpec((G, n, D), lambda i: (i, 0, 0))
    return pl.pallas_call(
        _router_kernel,
        grid=(NG // G,),
        in_specs=[grp(R), grp(1), grp(1),
                  pl.BlockSpec(g.shape, lambda i: (0, 0)),
                  pl.BlockSpec(wr3.shape, lambda i: (0, 0, 0))],
        out_specs=[pl.BlockSpec((tm, D), lambda i: (i, 0)),
                   pl.BlockSpec((tm, 128), lambda i: (i, 0)),
                   pl.BlockSpec((tm, 128), lambda i: (i, 0))],
        out_shape=[jax.ShapeDtypeStruct((T, D), F32), jax.ShapeDtypeStruct((T, 128), I32),
                   jax.ShapeDtypeStruct((T, 128), F32)],
        compiler_params=_cp("parallel"),
    )(x3, shift, scale, g, wr3)


def _row_copy(src_ref, src_row, dst_ref, dst_row, sem):
    return pltpu.make_async_copy(src_ref.at[pl.ds(src_row, 1), :], dst_ref.at[pl.ds(dst_row, 1), :], sem)


def _gather_kernel(idx_ref, src_ref, dst_ref, sem, *, rows):
    base = pl.program_id(0) * rows

    def issue(r, c):
        _row_copy(src_ref, idx_ref[base + r], dst_ref, base + r, sem).start()
        return c

    def drain(r, c):
        _row_copy(src_ref, 0, dst_ref, base + r, sem).wait()
        return c

    lax.fori_loop(0, rows, issue, 0)
    lax.fori_loop(0, rows, drain, 0)


def _gather_rows(idx, src, n_rows, *, rows):
    return pl.pallas_call(
        functools.partial(_gather_kernel, rows=rows),
        grid_spec=pltpu.PrefetchScalarGridSpec(
            num_scalar_prefetch=1,
            grid=(n_rows // rows,),
            in_specs=[pl.BlockSpec(memory_space=pl.ANY)],
            out_specs=pl.BlockSpec(memory_space=pl.ANY),
            scratch_shapes=[pltpu.SemaphoreType.DMA(())]),
        out_shape=jax.ShapeDtypeStruct((n_rows, src.shape[1]), src.dtype),
        compiler_params=_cp("arbitrary"),
    )(idx, src)


def _moe_ffn_kernel(te_ref, na_ref, xs_ref, wg_ref, wu_ref, wd_ref, y_ref, h_scr, acc_scr, *, nf):
    i, j = pl.program_id(0), pl.program_id(1)
    active = i < na_ref[0]

    @pl.when(jnp.logical_and(active, j == 0))
    def _():
        h_scr[...] = xs_ref[...].astype(BF16)
        acc_scr[...] = jnp.zeros_like(acc_scr)

    @pl.when(active)
    def _():
        h = h_scr[...]
        a = _dot(h, wg_ref[0, 0].astype(BF16))
        u = _dot(h, wu_ref[0, 0].astype(BF16))
        acc_scr[...] += _dot((_silu(a) * u).astype(BF16), wd_ref[0, 0].astype(BF16))

    @pl.when(j == nf - 1)
    def _():
        y_ref[...] = acc_scr[...]


def _moe_ffn(tile_expert, n_active, xs, wg, wu, wd, li, *, tmE, tf):
    R, D = xs.shape
    F = wg.shape[-1]
    nf = F // tf
    jj = lambda i, j, na: jnp.where(i < na[0], j, nf - 1)
    return pl.pallas_call(
        functools.partial(_moe_ffn_kernel, nf=nf),
        grid_spec=pltpu.PrefetchScalarGridSpec(
            num_scalar_prefetch=2,
            grid=(R // tmE, nf),
            in_specs=[pl.BlockSpec((tmE, D), lambda i, j, te, na: (i, 0)),
                      pl.BlockSpec((1, 1, D, tf), lambda i, j, te, na: (li, te[i], 0, jj(i, j, na))),
                      pl.BlockSpec((1, 1, D, tf), lambda i, j, te, na: (li, te[i], 0, jj(i, j, na))),
                      pl.BlockSpec((1, 1, tf, D), lambda i, j, te, na: (li, te[i], jj(i, j, na), 0))],
            out_specs=pl.BlockSpec((tmE, D), lambda i, j, te, na: (i, 0)),
            scratch_shapes=[pltpu.VMEM((tmE, D), BF16), pltpu.VMEM((tmE, D), F32)]),
        out_shape=jax.ShapeDtypeStruct((R, D), F32),
        compiler_params=_cp("arbitrary", "arbitrary"),
    )(tile_expert, n_active, xs, wg, wu, wd)


def _combine_kernel(pos_ref, y_ref, x_ref, gate_ref, gw_ref, o_ref, buf, sem, *, tmc):
    base = pl.program_id(0) * tmc
    G, R, D = x_ref.shape

    def issue(r, c):
        for s in range(2):
            _row_copy(y_ref, pos_ref[2 * (base + r) + s], buf.at[s], r, sem).start()
        return c

    def drain(r, c):
        for s in range(2):
            _row_copy(y_ref, 0, buf.at[s], r, sem).wait()
        return c

    lax.fori_loop(0, tmc, issue, 0)
    lax.fori_loop(0, tmc, drain, 0)
    gw = gw_ref[...]
    y = gw[:, 0:1] * buf[0] + gw[:, 1:2] * buf[1]
    o_ref[...] = x_ref[...] + gate_ref[...] * y.reshape(G, R, D)


def _combine(pos, y, x3, gate, gw, *, tmc):
    NG, R, D = x3.shape
    G = tmc // R
    return pl.pallas_call(
        functools.partial(_combine_kernel, tmc=tmc),
        grid_spec=pltpu.PrefetchScalarGridSpec(
            num_scalar_prefetch=1,
            grid=(NG // G,),
            in_specs=[pl.BlockSpec(memory_space=pl.ANY),
                      pl.BlockSpec((G, R, D), lambda i, p: (i, 0, 0)),
                      pl.BlockSpec((G, 1, D), lambda i, p: (i, 0, 0)),
                      pl.BlockSpec((tmc, 128), lambda i, p: (i, 0))],
            out_specs=pl.BlockSpec((G, R, D), lambda i, p: (i, 0, 0)),
            scratch_shapes=[pltpu.VMEM((2, tmc, D), F32), pltpu.SemaphoreType.DMA(())]),
        out_shape=jax.ShapeDtypeStruct(x3.shape, F32),
        compiler_params=_cp("arbitrary"),
    )(pos, y, x3, gate, gw)


def _moe_plan(idx2, *, tmE, n_tiles):
    T = idx2.shape[0]
    flat = idx2.reshape(-1)
    onehot = (flat[:, None] == jnp.arange(N_EXPERTS, dtype=I32)[None, :]).astype(I32)
    csum = jnp.cumsum(onehot, axis=0)
    rank = jnp.take_along_axis(csum, flat[:, None], axis=1)[:, 0] - 1
    counts = csum[-1]
    padded = ((counts + tmE - 1) // tmE) * tmE
    ends = jnp.cumsum(padded)
    starts = ends - padded
    pos = starts[flat] + rank
    src = jnp.zeros((n_tiles * tmE,), I32).at[pos].set(jnp.arange(2 * T, dtype=I32) // 2)
    n_active = (ends[-1] // tmE).astype(I32)
    tile_start = jnp.arange(n_tiles, dtype=I32) * tmE
    te = jnp.searchsorted(ends, tile_start, side="right").astype(I32)
    last = jnp.searchsorted(ends, (n_active - 1) * tmE, side="right").astype(I32)
    te = jnp.where(jnp.arange(n_tiles) < n_active, te, last)
    return pos.astype(I32), src, te, n_active.reshape(1)


def _moe(x3, shift, scale, gate, g, wr, wg, wu, wd, li, *, tm, tmE, tf, tmc, rows):
    NG, R, D = x3.shape
    T = NG * R
    wr_pad = jnp.zeros((D, 128), F32).at[:, :N_EXPERTS].set(wr)
    wr3 = jnp.stack(_split3(wr_pad))
    h, idx, gw = _router(x3, shift, scale, g, wr3, tm=tm)
    n_tiles = (2 * T) // tmE + N_EXPERTS
    pos, src, te, n_active = _moe_plan(idx[:, :2], tmE=tmE, n_tiles=n_tiles)
    xs = _gather_rows(src, h, n_tiles * tmE, rows=rows)
    y = _moe_ffn(te, n_active, xs, wg, wu, wd, li, tmE=tmE, tf=tf)
    return _combine(pos, y, x3, gate, gw, tmc=tmc)


def _final_kernel(x_ref, g_ref, o_ref):
    x = x_ref[...]
    ms = jnp.mean(x * x, axis=-1, keepdims=True)
    o_ref[...] = x * lax.rsqrt(ms + EPS) * g_ref[...]


def _final_norm(x2, g, *, tm):
    T, D = x2.shape
    return pl.pallas_call(
        _final_kernel,
        grid=(T // tm,),
        in_specs=[pl.BlockSpec((tm, D), lambda i: (i, 0)), pl.BlockSpec(g.shape, lambda i: (0, 0))],
        out_specs=pl.BlockSpec((tm, D), lambda i: (i, 0)),
        out_shape=jax.ShapeDtypeStruct((T, D), F32),
        compiler_params=_cp("parallel"),
    )(x2, g)


def _segment_sum_matrix():
    r = jnp.arange(GLA_HEADS * GLA_DK)[:, None] // GLA_DK
    c = jnp.arange(GLA_HEADS * GLA_DV)[None, :] // GLA_DV
    return (r == c).astype(BF16)


def _pack_in_weights(w_in_l, b_f_l, w_a2_l, b_a_l):
    D = w_in_l.shape[0]
    o_ff, o_g = 1536, 1544
    o_ga = o_g + 256 + 256 + 512 + 512
    w = jnp.concatenate([w_in_l[:, :o_ff], w_in_l[:, o_g:o_ga]], axis=1).astype(BF16)
    wm = jnp.zeros((D, 128), F32).at[:, :GLA_RANK].set(w_in_l[:, o_ga:o_ga + GLA_RANK]).astype(BF16)
    wf = jnp.zeros((16, D), F32).at[:FOX_HEADS].set(w_in_l[:, o_ff:o_g].T).astype(BF16)
    bf = jnp.zeros((16, 1), F32).at[:FOX_HEADS, 0].set(b_f_l)
    wa = jnp.zeros((128, 256), F32).at[:GLA_RANK].set(w_a2_l).astype(BF16)
    return w, wm, wf, bf, wa, b_a_l.reshape(1, -1)


def kernel(x_prompt, x_sample, cache_fox_k, cache_fox_v, cache_fox_lf, state_gla, c_prompt, c_sample, w_ada, b_ada, g_attn, g_ffn, w_in, b_fox_f, w_gla_a2, b_gla_a, g_gla, w_o, w_ffn_gate, w_ffn_up, w_ffn_down, w_router, w_moe_gate, w_moe_up, w_moe_down, g_final):
    B, S, D = x_prompt.shape
    NB, TN, _ = x_sample.shape
    L = w_in.shape[0]
    P = cache_fox_k.shape[2]
    TP, TS = B * S, NB * TN
    T = TP + TS
    NG = T // GROUP
    W = FOX_HEADS * FOX_DH
    KW = GLA_HEADS * GLA_DK
    tm = 512

    x3 = jnp.concatenate([x_prompt.reshape(TP, D), x_sample.reshape(TS, D)], axis=0).reshape(NG, GROUP, D)

    n_c = B + NB
    c_all = jnp.zeros((-(-n_c // 8) * 8, D), F32).at[:n_c].set(jnp.concatenate([c_prompt, c_sample], axis=0))
    mods = _ada(c_all, w_ada, b_ada)[:, :n_c].reshape(L, n_c, 6, D)
    mods = jnp.concatenate([jnp.repeat(mods[:, :B], S // GROUP, axis=1), mods[:, B:]], axis=1)

    segb = _segment_sum_matrix()
    kc_all = cache_fox_k.reshape(L, NB, P, W)
    vc_all = cache_fox_v.reshape(L, NB, P, W)
    lfc_all = jnp.swapaxes(cache_fox_lf, 2, 3)
    s_prompt0 = jnp.zeros((B, KW, GLA_DV), F32)

    outs = {n: [] for n in ("kp", "vp", "lp", "sp", "kn", "vn", "ln", "sn")}
    for l in range(L):
        m = [mods[l, :, j][:, None, :] for j in range(6)]
        w, wm, wf, bf, wa, ba = _pack_in_weights(w_in[l], b_fox_f[l], w_gla_a2[l], b_gla_a[l])
        (fq, fk32, fv32, fkb, fvb, gq, gk, gv, gr, la, lft, ft) = _inproj(
            x3, m[0], m[1], g_attn[l].reshape(1, D), w, wm, wf, bf, wa, ba, tm=tm, blocks_per_seq=S // tm)

        ft4 = ft[:FOX_HEADS, :TP].reshape(FOX_HEADS // 2, 2, B, S).transpose(2, 0, 1, 3)
        fo_p = _fox_prompt(fq, fkb, fvb, ft4, B=B, S=S, tq=512, tk=512)
        lfn3 = lft[:FOX_HEADS, TP:].reshape(FOX_HEADS, NB, TN).transpose(1, 0, 2)
        fo_s = _fox_sample(fq[TP:].reshape(NB, TN, W), fkb[TP:].reshape(NB, TN, W), fvb[TP:].reshape(NB, TN, W),
                           lfn3, kc_all[l], vc_all[l], lfc_all[l], tk=512)

        go_p, s_p = _gla(gq, gk, gv, la, s_prompt0, segb, C=64, nseq=B, nchunk=S // 64, row_block_off=0)
        go_s, s_n = _gla(gq, gk, gv, la, state_gla[l].reshape(NB, KW, GLA_DV), segb,
                         C=TN, nseq=NB, nchunk=1, row_block_off=TP // TN)

        x3 = _outproj(fo_p, fo_s, go_p, go_s, gr, x3, m[2], g_gla[l].reshape(1, GLA_DV), w_o[l].astype(BF16), tm=tm)

        if l % 2 == 0:
            x3 = _ffn(x3, m[3], m[4], m[5], g_ffn[l].reshape(1, D), w_ffn_gate, w_ffn_up, w_ffn_down, l // 2,
                      tm=1024, tf=256)
        else:
            x3 = _moe(x3, m[3], m[4], m[5], g_ffn[l].reshape(1, D), w_router[l // 2], w_moe_gate, w_moe_up,
                      w_moe_down, l // 2, tm=tm, tmE=1024, tf=256, tmc=256, rows=1024)

        lf_tok = lft[:FOX_HEADS].T
        outs["kp"].append(fk32[:TP].reshape(B, S, FOX_HEADS, FOX_DH))
        outs["vp"].append(fv32[:TP].reshape(B, S, FOX_HEADS, FOX_DH))
        outs["lp"].append(lf_tok[:TP].reshape(B, S, FOX_HEADS))
        outs["sp"].append(s_p.reshape(B, GLA_HEADS, GLA_DK, GLA_DV))
        outs["kn"].append(fk32[TP:].reshape(NB, TN, FOX_HEADS, FOX_DH))
        outs["vn"].append(fv32[TP:].reshape(NB, TN, FOX_HEADS, FOX_DH))
        outs["ln"].append(lf_tok[TP:].reshape(NB, TN, FOX_HEADS))
        outs["sn"].append(s_n.reshape(NB, GLA_HEADS, GLA_DK, GLA_DV))

    y = _final_norm(x3.reshape(T, D), g_final.reshape(1, D), tm=tm)
    st = lambda n: jnp.stack(outs[n])
    return (y[:TP].reshape(B, S, D), y[TP:].reshape(NB, TN, D),
            st("kp"), st("vp"), st("lp"), st("sp"), st("kn"), st("vn"), st("ln"), st("sn"))
```

```python
import functools
import math

import jax
import jax.numpy as jnp
from jax import lax
from jax.experimental import pallas as pl
from jax.experimental.pallas import tpu as pltpu

F32, BF16, I32 = jnp.float32, jnp.bfloat16, jnp.int32
EPS = 1e-6
NEG = -1e30
LOG2E = math.log2(math.e)
GROUP = 32
FOX_HEADS, FOX_DH = 8, 64
GLA_HEADS, GLA_DK, GLA_DV = 4, 64, 128
GLA_RANK = 16
GLA_TAU = 16.0
N_EXPERTS = 8
SUB = 16
FF_LANE = 16
VMEM_LIMIT = 56 * 1024 * 1024


def _cp(*sem):
    return pltpu.CompilerParams(dimension_semantics=sem, vmem_limit_bytes=VMEM_LIMIT)


def _dot(a, b):
    return jnp.dot(a, b, preferred_element_type=F32)


def _dot_nt(a, b):
    return lax.dot_general(a, b, (((1,), (1,)), ((), ())), preferred_element_type=F32)


def _dot_tn(a, b):
    return lax.dot_general(a, b, (((0,), (0,)), ((), ())), preferred_element_type=F32)


def _split3(x):
    h = x.astype(BF16)
    r = x - h.astype(F32)
    m = r.astype(BF16)
    l = (r - m.astype(F32)).astype(BF16)
    return h, m, l


def _dot3(x, u):
    h, m, l = _split3(x)
    return _dot(h, u) + _dot(m, u) + _dot(l, u)


def _dot3_l(u, x):
    h, m, l = _split3(x)
    return _dot(u, h) + _dot(u, m) + _dot(u, l)


def _dot3_tn(x, u):
    h, m, l = _split3(x)
    return _dot_tn(h, u) + _dot_tn(m, u) + _dot_tn(l, u)


def _silu(x):
    return x * (1.0 / (1.0 + jnp.exp(-x)))


def _log_sigmoid(x):
    return jnp.minimum(x, 0.0) - jnp.log1p(jnp.exp(-jnp.abs(x)))


def _norm_mod(x3, g, shift, scale):
    ms = jnp.mean(x3 * x3, axis=-1, keepdims=True)
    return (x3 * lax.rsqrt(ms + EPS) * g) * (1.0 + scale) + shift


def _iota(shape, axis):
    return lax.broadcasted_iota(I32, shape, axis)


def _ones_where(cond):
    return jnp.where(cond, 1.0, 0.0).astype(BF16)


def _ada_kernel(c_ref, w_ref, b_ref, o_ref):
    a = _silu(c_ref[...]).astype(BF16)
    o_ref[0] = _dot(a, w_ref[0].astype(BF16)) + b_ref[0]


def _ada(c_all, w_ada, b_ada):
    L, D, N = w_ada.shape
    R = c_all.shape[0]
    tn = 1536
    return pl.pallas_call(
        _ada_kernel,
        grid=(L, N // tn),
        in_specs=[pl.BlockSpec((R, D), lambda l, j: (0, 0)),
                  pl.BlockSpec((1, D, tn), lambda l, j: (l, 0, j)),
                  pl.BlockSpec((1, 1, tn), lambda l, j: (l, 0, j))],
        out_specs=pl.BlockSpec((1, R, tn), lambda l, j: (l, 0, j)),
        out_shape=jax.ShapeDtypeStruct((L, R, N), F32),
        compiler_params=_cp("parallel", "parallel"),
    )(c_all, w_ada, b_ada.reshape(L, 1, N))


def _inproj_kernel(x_ref, sh_ref, sc_ref, g_ref, w_ref, wvt_ref, wm_ref, bm_ref, wa_ref, ba_ref, place_ref, one_ref,
                   fq_ref, fk32_ref, fv32_ref, fkb_ref, vt_ref, qx_ref, kx_ref, gq_ref, gk_ref, gv_ref, gr_ref,
                   la_ref, lf_ref, carry_ref, *, blocks_per_seq):
    i = pl.program_id(0)
    G, R, D = x_ref.shape
    tm = G * R
    h = _norm_mod(x_ref[...], g_ref[...], sh_ref[...], sc_ref[...]).reshape(tm, D).astype(BF16)

    fq_ref[...] = (_dot(h, w_ref[:, 0:512]) * (LOG2E * FOX_DH ** -0.5)).astype(BF16)
    z = _dot(h, w_ref[:, 512:1024])
    fk32_ref[...] = z
    fkb_ref[...] = z.astype(BF16)
    fv32_ref[...] = _dot(h, w_ref[:, 1024:1536])
    vt_ref[...] = _dot_nt(wvt_ref[...], h).astype(BF16)
    gq_ref[...] = _dot(h, w_ref[:, 1536:1792]) * (GLA_DK ** -0.5)
    gk_ref[...] = _dot(h, w_ref[:, 1792:2048])
    gv_ref[...] = _dot(h, w_ref[:, 2048:2560]).astype(BF16)
    gr_ref[...] = _dot(h, w_ref[:, 2560:3072])

    zm = _dot(h, wm_ref[...])
    la_pre = _dot(zm.astype(BF16), wa_ref[...]) + ba_ref[...]
    la_ref[...] = _log_sigmoid(la_pre) * (1.0 / GLA_TAU)
    lf = _log_sigmoid(zm + bm_ref[...])
    lf_ref[...] = lf

    @pl.when(i % blocks_per_seq == 0)
    def _():
        carry_ref[...] = jnp.zeros_like(carry_ref)

    half = tm // 2
    ltri = _ones_where(_iota((half, half), 1) <= _iota((half, half), 0))
    parts = jnp.concatenate(_split3(lf), axis=1)

    def fold(c):
        return c[:, 0:128] + c[:, 128:256] + c[:, 256:384]

    f0 = fold(_dot(ltri, parts[:half])) + carry_ref[0:1, :]
    f1 = fold(_dot(ltri, parts[half:])) + f0[half - 1:half, :]
    carry_ref[...] = jnp.broadcast_to(f1[half - 1:half, :], carry_ref.shape)
    f = jnp.concatenate([f0, f1], axis=0) * LOG2E

    xk = _dot(jnp.concatenate(_split3(f), axis=1), place_ref[...]) + one_ref[...]
    qx_ref[...] = xk[:, 0:128].astype(BF16)
    kx_ref[...] = xk[:, 128:256].astype(BF16)


def _inproj(x3, shift, scale, g, w, wvt, wm, bm, wa, ba, place, one, *, tm, blocks_per_seq):
    NG, R, D = x3.shape
    T = NG * R
    G = tm // R
    nb = T // tm
    row = lambda n: pl.BlockSpec((tm, n), lambda i: (i, 0))
    full = lambda a: pl.BlockSpec(a.shape, lambda i: (0,) * a.ndim)
    outs = [
        (jax.ShapeDtypeStruct((T, 512), BF16), row(512)),
        (jax.ShapeDtypeStruct((T, 512), F32), row(512)),
        (jax.ShapeDtypeStruct((T, 512), F32), row(512)),
        (jax.ShapeDtypeStruct((T, 512), BF16), row(512)),
        (jax.ShapeDtypeStruct((512, T), BF16), pl.BlockSpec((512, tm), lambda i: (0, i))),
        (jax.ShapeDtypeStruct((T, 128), BF16), row(128)),
        (jax.ShapeDtypeStruct((T, 128), BF16), row(128)),
        (jax.ShapeDtypeStruct((T, 256), F32), row(256)),
        (jax.ShapeDtypeStruct((T, 256), F32), row(256)),
        (jax.ShapeDtypeStruct((T, 512), BF16), row(512)),
        (jax.ShapeDtypeStruct((T, 512), F32), row(512)),
        (jax.ShapeDtypeStruct((T, 256), F32), row(256)),
        (jax.ShapeDtypeStruct((T, 128), F32), row(128)),
    ]
    return pl.pallas_call(
        functools.partial(_inproj_kernel, blocks_per_seq=blocks_per_seq),
        grid=(nb,),
        in_specs=[pl.BlockSpec((G, R, D), lambda i: (i, 0, 0)),
                  pl.BlockSpec((G, 1, D), lambda i: (i, 0, 0)),
                  pl.BlockSpec((G, 1, D), lambda i: (i, 0, 0)),
                  full(g), full(w), full(wvt), full(wm), full(bm), full(wa), full(ba), full(place), full(one)],
        out_specs=[o[1] for o in outs],
        out_shape=[o[0] for o in outs],
        scratch_shapes=[pltpu.VMEM((8, 128), F32)],
        compiler_params=_cp("arbitrary"),
    )(x3, shift, scale, g, w, wvt, wm, bm, wa, ba, place, one)


def _fox_prompt_kernel(q_ref, qx_ref, k_ref, kx_ref, vt_ref, o_ref, m_scr, l_scr, acc_scr, *, tq, tk):
    p = pl.program_id(1)
    i = pl.program_id(2)
    lane = _iota((1, 128), 1)
    q2 = q_ref[...]
    qx = qx_ref[...]
    zero = jnp.zeros_like(q2)
    qa = []
    for hh in range(2):
        mine = (lane < FOX_DH) if hh == 0 else (lane >= FOX_DH)
        qa.append(jnp.concatenate([jnp.where(mine, q2, zero),
                                   jnp.where(lane // 16 == 2 * p + hh, qx, zero)], axis=1))
    q0 = i * tq
    m_scr[...] = jnp.full(m_scr.shape, NEG, F32)
    l_scr[...] = jnp.zeros_like(l_scr)
    acc_scr[...] = jnp.zeros_like(acc_scr)

    def step(kb, masked):
        k0 = pl.multiple_of(kb * tk, tk)
        ka = jnp.concatenate([k_ref[pl.ds(k0, tk), :], kx_ref[pl.ds(k0, tk), :]], axis=1)
        for hh in range(2):
            st = _dot_nt(ka, qa[hh])
            if masked:
                visible = (k0 + _iota((tk, tq), 0)) <= (q0 + _iota((tk, tq), 1))
                st = jnp.where(visible, st, NEG)
            m_old = m_scr[hh]
            m_new = jnp.maximum(m_old, jnp.max(st, axis=0, keepdims=True))
            alpha = jnp.exp2(m_old - m_new)
            pt = jnp.exp2(st - m_new)
            l_scr[hh] = alpha * l_scr[hh] + jnp.sum(pt, axis=0, keepdims=True)
            m_scr[hh] = m_new
            rows = slice(FOX_DH * hh, FOX_DH * (hh + 1))
            acc_scr[rows, :] = acc_scr[rows, :] * alpha + _dot(vt_ref[rows, pl.ds(k0, tk)], pt.astype(BF16))

    n_full = (i * tq) // tk

    def body(kb, c):
        step(kb, False)
        return c

    lax.fori_loop(0, n_full, body, 0)
    for d in range(tq // tk):
        step(n_full + d, True)
    inv = jnp.concatenate([jnp.broadcast_to(1.0 / l_scr[hh], (FOX_DH, tq)) for hh in range(2)], axis=0)
    o_ref[...] = jnp.transpose(acc_scr[...] * inv).astype(o_ref.dtype)


def _fox_prompt(fq, qx, fkb, kx, vt, *, B, S, tq, tk):
    nq = S // tq
    return pl.pallas_call(
        functools.partial(_fox_prompt_kernel, tq=tq, tk=tk),
        grid=(B, FOX_HEADS // 2, nq),
        in_specs=[pl.BlockSpec((tq, 128), lambda b, p, i: (b * nq + i, p)),
                  pl.BlockSpec((tq, 128), lambda b, p, i: (b * nq + i, 0)),
                  pl.BlockSpec((S, 128), lambda b, p, i: (b, p)),
                  pl.BlockSpec((S, 128), lambda b, p, i: (b, 0)),
                  pl.BlockSpec((128, S), lambda b, p, i: (p, b))],
        out_specs=pl.BlockSpec((tq, 128), lambda b, p, i: (b * nq + i, p)),
        out_shape=jax.ShapeDtypeStruct((B * S, 512), BF16),
        scratch_shapes=[pltpu.VMEM((2, 1, tq), F32), pltpu.VMEM((2, 1, tq), F32),
                        pltpu.VMEM((128, tq), F32)],
        compiler_params=_cp("parallel", "parallel", "arbitrary"),
    )(fq, qx, fkb, kx, vt)


def _fox_sample_kernel(q_ref, kn_ref, vn_ref, lfn_ref, kc_ref, vc_ref, lfc_ref, o_ref,
                       qbd_scr, fn_scr, m_scr, l_scr, acc_scr, g_scr, *, tk, nkb):
    kb = pl.program_id(1)
    TN = q_ref.shape[1]
    HT = FOX_HEADS * TN
    W = FOX_HEADS * FOX_DH

    def online(s, vblk):
        m_old = m_scr[...]
        m_new = jnp.maximum(m_old, jnp.max(s, axis=1, keepdims=True))
        alpha = jnp.exp2(m_old - m_new)
        p = jnp.exp2(s - m_new)
        l_scr[...] = alpha * l_scr[...] + jnp.sum(p, axis=1, keepdims=True)
        m_scr[...] = m_new
        acc_scr[...] = acc_scr[...] * alpha + _dot(p.astype(BF16), vblk)

    @pl.when(kb == 0)
    def _():
        q = q_ref[0]
        qt = jnp.concatenate([q] * FOX_HEADS, axis=0)
        same = (_iota((HT, W), 0) // TN) == (_iota((HT, W), 1) // FOX_DH)
        qbd = jnp.where(same, qt, jnp.zeros_like(qt))
        qbd_scr[...] = qbd
        tri = _ones_where(_iota((TN, TN), 0) <= _iota((TN, TN), 1))
        fnt = _dot3(lfn_ref[0] * LOG2E, tri)
        rep = jnp.broadcast_to(fnt[:, None, :], (FOX_HEADS, TN, TN)).reshape(HT, TN)
        tpos = _iota((HT, TN), 0) % TN
        jpos = _iota((HT, TN), 1)
        fn_col = jnp.sum(jnp.where(tpos == jpos, rep, 0.0), axis=1, keepdims=True)
        fn_scr[...] = fn_col
        m_scr[...] = jnp.full(m_scr.shape, NEG, F32)
        l_scr[...] = jnp.zeros_like(l_scr)
        acc_scr[...] = jnp.zeros_like(acc_scr)
        g_scr[...] = jnp.zeros_like(g_scr)
        s = _dot_nt(qbd, kn_ref[0]) + fn_col - rep
        online(jnp.where(jpos <= tpos, s, NEG), vn_ref[0])

    lfc = lfc_ref[0] * LOG2E
    later = _ones_where(_iota((tk, tk), 0) > _iota((tk, tk), 1))
    gl = _dot3(lfc, later) + g_scr[:, 0:1]
    g_scr[...] = jnp.broadcast_to(gl[:, 0:1] + lfc[:, 0:1], g_scr.shape)
    grep = jnp.broadcast_to(gl[:, None, :], (FOX_HEADS, TN, tk)).reshape(HT, tk)
    s = _dot_nt(qbd_scr[...], kc_ref[0].astype(BF16)) + fn_scr[...] + grep
    online(s, vc_ref[0].astype(BF16))

    @pl.when(kb == nkb - 1)
    def _():
        accn = acc_scr[...] * (1.0 / l_scr[...])
        lane_head = _iota((TN, W), 1) // FOX_DH
        out = jnp.zeros((TN, W), F32)
        for h in range(FOX_HEADS):
            out = out + jnp.where(lane_head == h, accn[h * TN:(h + 1) * TN, :], 0.0)
        o_ref[...] = out.astype(o_ref.dtype)


def _fox_sample(q3, kn3, vn3, lfn3, kc, vc, lfc_t, *, tk):
    NB, TN, W = q3.shape
    P = kc.shape[1]
    nkb = P // tk
    HT = FOX_HEADS * TN
    new = lambda: pl.BlockSpec((1, TN, W), lambda b, k: (b, 0, 0))
    return pl.pallas_call(
        functools.partial(_fox_sample_kernel, tk=tk, nkb=nkb),
        grid=(NB, nkb),
        in_specs=[new(), new(), new(),
                  pl.BlockSpec((1, FOX_HEADS, TN), lambda b, k: (b, 0, 0)),
                  pl.BlockSpec((1, tk, W), lambda b, k: (b, nkb - 1 - k, 0)),
                  pl.BlockSpec((1, tk, W), lambda b, k: (b, nkb - 1 - k, 0)),
                  pl.BlockSpec((1, FOX_HEADS, tk), lambda b, k: (b, 0, nkb - 1 - k))],
        out_specs=pl.BlockSpec((TN, W), lambda b, k: (b, 0)),
        out_shape=jax.ShapeDtypeStruct((NB * TN, W), BF16),
        scratch_shapes=[pltpu.VMEM((HT, W), BF16), pltpu.VMEM((HT, 1), F32), pltpu.VMEM((HT, 1), F32),
                        pltpu.VMEM((HT, 1), F32), pltpu.VMEM((HT, W), F32), pltpu.VMEM((FOX_HEADS, 128), F32)],
        compiler_params=_cp("parallel", "arbitrary"),
    )(q3, kn3, vn3, lfn3, kc, vc, lfc_t)


def _gla_intra(q, k, v, la, segb, tri, ones_cv):
    C = q.shape[0]
    KW, VW = GLA_HEADS * GLA_DK, GLA_HEADS * GLA_DV
    b = _dot3_l(tri, la)
    blast = b[C - 1:C, :]
    qe = (q * jnp.exp(b)).astype(BF16)
    kd = (k * jnp.exp(blast - b)).astype(BF16)
    blocks = []
    for blk in range(C // SUB):
        r0 = blk * SUB
        bi, qi, ki = b[r0:r0 + SUB], q[r0:r0 + SUB], k[r0:r0 + SUB]
        diff = bi[None, :, :] - bi[:, None, :]
        causal = _iota((SUB, SUB, 1), 1) >= _iota((SUB, SUB, 1), 0)
        tm3 = jnp.exp(jnp.where(causal, diff, -jnp.inf)) * qi[None, :, :] * ki[:, None, :]
        arep = _dot(tm3.reshape(SUB * SUB, KW).astype(BF16), segb)
        vi = v[r0:r0 + SUB].astype(F32)
        o_blk = jnp.sum(arep.reshape(SUB, SUB, VW) * vi[:, None, :], axis=0)
        if blk > 0:
            bref = b[r0 - 1:r0]
            qs = (qi * jnp.exp(bi - bref)).astype(BF16)
            ks = (k[:r0] * jnp.exp(bref - b[:r0])).astype(BF16)
            parts = []
            for h in range(GLA_HEADS):
                ks_, vs_ = slice(h * GLA_DK, (h + 1) * GLA_DK), slice(h * GLA_DV, (h + 1) * GLA_DV)
                a = _dot_nt(qs[:, ks_], ks[:, ks_])
                parts.append(_dot(a.astype(BF16), v[:r0, vs_]))
            o_blk = o_blk + jnp.concatenate(parts, axis=1)
        blocks.append(o_blk)
    o_intra = jnp.concatenate(blocks, axis=0)
    dec = jnp.exp(_dot3_tn(la, ones_cv))
    upd = _dot_tn(kd, v)
    upd = jnp.concatenate([upd[h * GLA_DK:(h + 1) * GLA_DK, h * GLA_DV:(h + 1) * GLA_DV]
                           for h in range(GLA_HEADS)], axis=0)
    return qe, o_intra, dec, upd


def _gla_kernel(q_ref, k_ref, v_ref, la_ref, s0_ref, segb_ref, o_ref, sfin_ref, *scr, C, NC, chain, nsteps):
    if chain:
        (s_scr,) = scr

        @pl.when(pl.program_id(1) == 0)
        def _():
            s_scr[...] = s0_ref[0]

    tri = _ones_where(_iota((C, C), 1) <= _iota((C, C), 0))
    ones_cv = jnp.ones((C, GLA_DV), BF16)
    segb = segb_ref[...]
    units = []
    for u in range(NC):
        rows = slice(u * C, (u + 1) * C)
        units.append(_gla_intra(q_ref[rows, :], k_ref[rows, :], v_ref[rows, :], la_ref[rows, :], segb, tri, ones_cv))

    state = s_scr[...] if chain else None
    for u, (qe, o_intra, dec, upd) in enumerate(units):
        if not chain:
            state = s0_ref[u]
        s_bf = state.astype(BF16)
        o_inter = jnp.concatenate(
            [_dot(qe[:, h * GLA_DK:(h + 1) * GLA_DK], s_bf[h * GLA_DK:(h + 1) * GLA_DK, :]) for h in range(GLA_HEADS)],
            axis=1)
        o_ref[u * C:(u + 1) * C, :] = o_intra + o_inter
        state = dec * state + upd
        if not chain:
            sfin_ref[u] = state
    if chain:
        s_scr[...] = state

        @pl.when(pl.program_id(1) == nsteps - 1)
        def _():
            sfin_ref[0] = state


def _gla(gq, gk, gv, la, s0, segb, *, C, NC, chain, nseq, nsteps, row_block_off):
    KW, VW = GLA_HEADS * GLA_DK, GLA_HEADS * GLA_DV
    rows = NC * C
    ns = 1 if chain else NC
    tok = lambda n: pl.BlockSpec((rows, n), lambda s, c: (row_block_off + s * nsteps + c, 0))
    return pl.pallas_call(
        functools.partial(_gla_kernel, C=C, NC=NC, chain=chain, nsteps=nsteps),
        grid=(nseq, nsteps),
        in_specs=[tok(KW), tok(KW), tok(VW), tok(KW),
                  pl.BlockSpec((ns, KW, GLA_DV), lambda s, c: (s, 0, 0)),
                  pl.BlockSpec(segb.shape, lambda s, c: (0, 0))],
        out_specs=[pl.BlockSpec((rows, VW), lambda s, c: (s * nsteps + c, 0)),
                   pl.BlockSpec((ns, KW, GLA_DV), lambda s, c: (s, 0, 0))],
        out_shape=[jax.ShapeDtypeStruct((nseq * nsteps * rows, VW), F32),
                   jax.ShapeDtypeStruct((nseq * ns, KW, GLA_DV), F32)],
        scratch_shapes=[pltpu.VMEM((KW, GLA_DV), F32)] if chain else [],
        compiler_params=_cp("parallel", "arbitrary"),
    )(gq, gk, gv, la, s0, segb)


def _outproj_kernel(fop_ref, fos_ref, gop_ref, gos_ref, gr_ref, x_ref, gate_ref, gg_ref, wo_ref, o_ref, *, nbp):
    G, R, D = x_ref.shape
    is_prompt = pl.program_id(0) < nbp
    fo = jnp.where(is_prompt, fop_ref[...], fos_ref[...])
    go = jnp.where(is_prompt, gop_ref[...], gos_ref[...])
    parts = []
    for h in range(GLA_HEADS):
        seg = go[:, h * GLA_DV:(h + 1) * GLA_DV]
        ms = jnp.mean(seg * seg, axis=-1, keepdims=True)
        parts.append(seg * lax.rsqrt(ms + EPS) * gg_ref[...])
    gn = jnp.concatenate(parts, axis=1) * _silu(gr_ref[...])
    mix = _dot(fo, wo_ref[0:512, :]) + _dot(gn.astype(BF16), wo_ref[512:1024, :])
    o_ref[...] = x_ref[...] + gate_ref[...] * mix.reshape(G, R, D)


def _outproj(fo_p, fo_s, go_p, go_s, gr, x3, gate, gg, wo, *, tm):
    NG, R, D = x3.shape
    G = tm // R
    nbp = fo_p.shape[0] // tm
    row = lambda n: pl.BlockSpec((tm, n), lambda i: (i, 0))
    prow = lambda n: pl.BlockSpec((tm, n), lambda i: (jnp.minimum(i, nbp - 1), 0))
    srow = lambda n: pl.BlockSpec((tm, n), lambda i: (jnp.maximum(i - nbp, 0), 0))
    return pl.pallas_call(
        functools.partial(_outproj_kernel, nbp=nbp),
        grid=(NG // G,),
        in_specs=[prow(512), srow(512), prow(512), srow(512), row(512),
                  pl.BlockSpec((G, R, D), lambda i: (i, 0, 0)),
                  pl.BlockSpec((G, 1, D), lambda i: (i, 0, 0)),
                  pl.BlockSpec(gg.shape, lambda i: (0, 0)),
                  pl.BlockSpec(wo.shape, lambda i: (0, 0))],
        out_specs=pl.BlockSpec((G, R, D), lambda i: (i, 0, 0)),
        out_shape=jax.ShapeDtypeStruct(x3.shape, F32),
        compiler_params=_cp("parallel"),
    )(fo_p, fo_s, go_p, go_s, gr, x3, gate, gg, wo)


def _ffn_kernel(x_ref, sh_ref, sc_ref, gate_ref, g_ref, wg_ref, wu_ref, wd_ref, o_ref, h_scr, acc_scr, *, nf):
    j = pl.program_id(1)
    G, R, D = x_ref.shape

    @pl.when(j == 0)
    def _():
        h_scr[...] = _norm_mod(x_ref[...], g_ref[...], sh_ref[...], sc_ref[...]).reshape(G * R, D).astype(BF16)
        acc_scr[...] = jnp.zeros_like(acc_scr)

    h = h_scr[...]
    a = _dot(h, wg_ref[0].astype(BF16))
    u = _dot(h, wu_ref[0].astype(BF16))
    acc_scr[...] += _dot((_silu(a) * u).astype(BF16), wd_ref[0].astype(BF16))

    @pl.when(j == nf - 1)
    def _():
        o_ref[...] = x_ref[...] + gate_ref[...] * acc_scr[...].reshape(G, R, D)


def _ffn(x3, shift, scale, gate, g, wg, wu, wd, li, *, tm, tf):
    NG, R, D = x3.shape
    G = tm // R
    F = wg.shape[-1]
    nf = F // tf
    grp = lambda n: pl.BlockSpec((G, n, D), lambda i, j: (i, 0, 0))
    return pl.pallas_call(
        functools.partial(_ffn_kernel, nf=nf),
        grid=(NG // G, nf),
        in_specs=[grp(R), grp(1), grp(1), grp(1),
                  pl.BlockSpec(g.shape, lambda i, j: (0, 0)),
                  pl.BlockSpec((1, D, tf), lambda i, j: (li, 0, j)),
                  pl.BlockSpec((1, D, tf), lambda i, j: (li, 0, j)),
                  pl.BlockSpec((1, tf, D), lambda i, j: (li, j, 0))],
        out_specs=grp(R),
        out_shape=jax.ShapeDtypeStruct(x3.shape, F32),
        scratch_shapes=[pltpu.VMEM((tm, D), BF16), pltpu.VMEM((tm, D), F32)],
        compiler_params=_cp("parallel", "arbitrary"),
    )(x3, shift, scale, gate, g, wg, wu, wd)


def _router_kernel(x_ref, sh_ref, sc_ref, g_ref, wr_ref, h_ref, idx_ref, gw_ref):
    G, R, D = x_ref.shape
    h = _norm_mod(x_ref[...], g_ref[...], sh_ref[...], sc_ref[...]).reshape(G * R, D)
    h_ref[...] = h
    hh, hm, hl = _split3(h)
    wh, wm, wl = wr_ref[0], wr_ref[1], wr_ref[2]
    logits = (_dot(hh, wh) + _dot(hh, wm) + _dot(hm, wh)
              + _dot(hh, wl) + _dot(hm, wm) + _dot(hl, wh))
    lane = _iota(logits.shape, 1)
    logits = jnp.where(lane < N_EXPERTS, logits, -jnp.inf)
    v1 = jnp.max(logits, axis=1, keepdims=True)
    i1 = jnp.min(jnp.where(logits == v1, lane, 128), axis=1, keepdims=True)
    rest = jnp.where(lane == i1, -jnp.inf, logits)
    v2 = jnp.max(rest, axis=1, keepdims=True)
    i2 = jnp.min(jnp.where(rest == v2, lane, 128), axis=1, keepdims=True)
    e = jnp.exp(v2 - v1)
    g1 = 1.0 / (1.0 + e)
    idx_ref[...] = jnp.where(lane == 0, i1, jnp.where(lane == 1, i2, 0))
    gw_ref[...] = jnp.where(lane == 0, g1, jnp.where(lane == 1, e * g1, 0.0))


def _router(x3, shift, scale, g, wr3, *, tm):
    NG, R, D = x3.shape
    T = NG * R
    G = tm // R
    grp = lambda n: pl.BlockSpec((G, n, D), lambda i: (i, 0, 0))
    return pl.pallas_call(
        _router_kernel,
        grid=(NG // G,),
        in_specs=[grp(R), grp(1), grp(1),
                  pl.BlockSpec(g.shape, lambda i: (0, 0)),
                  pl.BlockSpec(wr3.shape, lambda i: (0, 0, 0))],
        out_specs=[pl.BlockSpec((tm, D), lambda i: (i, 0)),
                   pl.BlockSpec((tm, 128), lambda i: (i, 0)),
                   pl.BlockSpec((tm, 128), lambda i: (i, 0))],
        out_shape=[jax.ShapeDtypeStruct((T, D), F32), jax.ShapeDtypeStruct((T, 128), I32),
                   jax.ShapeDtypeStruct((T, 128), F32)],
        compiler_params=_cp("parallel"),
    )(x3, shift, scale, g, wr3)


def _row_copy(src_ref, src_row, dst_ref, dst_row, sem):
    return pltpu.make_async_copy(src_ref.at[pl.ds(src_row, 1), :], dst_ref.at[pl.ds(dst_row, 1), :], sem)


def _moe_ffn_kernel(te_ref, na_ref, src_ref, h_hbm, wg_ref, wu_ref, wd_ref, y_ref, xbuf, h_scr, acc_scr, sem,
                    *, nf, tmE, chunk):
    del te_ref
    i, j = pl.program_id(0), pl.program_id(1)
    na = na_ref[0]
    active = i < na
    slot = i % 2

    @pl.when(jnp.logical_and(i == 0, j == 0))
    def _():
        def issue(r, c):
            _row_copy(h_hbm, src_ref[r], xbuf.at[0], r, sem.at[0]).start()
            return c

        lax.fori_loop(0, tmE, issue, 0)

    @pl.when(jnp.logical_and(active, j == 0))
    def _():
        pltpu.make_async_copy(h_hbm.at[pl.ds(0, tmE), :], xbuf.at[slot], sem.at[slot]).wait()
        h_scr[...] = xbuf[slot].astype(BF16)
        acc_scr[...] = jnp.zeros_like(acc_scr)

    @pl.when(jnp.logical_and(i + 1 < na, j < tmE // chunk))
    def _():
        r0 = j * chunk
        base = (i + 1) * tmE + r0
        for u in range(chunk):
            _row_copy(h_hbm, src_ref[base + u], xbuf.at[1 - slot], r0 + u, sem.at[1 - slot]).start()

    @pl.when(active)
    def _():
        h = h_scr[...]
        a = _dot(h, wg_ref[0, 0].astype(BF16))
        u = _dot(h, wu_ref[0, 0].astype(BF16))
        acc_scr[...] += _dot((_silu(a) * u).astype(BF16), wd_ref[0, 0].astype(BF16))

    @pl.when(j == nf - 1)
    def _():
        y_ref[...] = acc_scr[...]


def _moe_ffn(tile_expert, n_active, src, h, wg, wu, wd, li, *, n_tiles, tmE, tf, chunk):
    T, D = h.shape
    F = wg.shape[-1]
    nf = F // tf
    assert tmE % chunk == 0 and tmE // chunk <= nf
    jj = lambda i, j, na: jnp.where(i < na[0], j, nf - 1)
    return pl.pallas_call(
        functools.partial(_moe_ffn_kernel, nf=nf, tmE=tmE, chunk=chunk),
        grid_spec=pltpu.PrefetchScalarGridSpec(
            num_scalar_prefetch=3,
            grid=(n_tiles, nf),
            in_specs=[pl.BlockSpec(memory_space=pl.ANY),
                      pl.BlockSpec((1, 1, D, tf), lambda i, j, te, na, sr: (li, te[i], 0, jj(i, j, na))),
                      pl.BlockSpec((1, 1, D, tf), lambda i, j, te, na, sr: (li, te[i], 0, jj(i, j, na))),
                      pl.BlockSpec((1, 1, tf, D), lambda i, j, te, na, sr: (li, te[i], jj(i, j, na), 0))],
            out_specs=pl.BlockSpec((tmE, D), lambda i, j, te, na, sr: (i, 0)),
            scratch_shapes=[pltpu.VMEM((2, tmE, D), F32), pltpu.VMEM((tmE, D), BF16), pltpu.VMEM((tmE, D), F32),
                            pltpu.SemaphoreType.DMA((2,))]),
        out_shape=jax.ShapeDtypeStruct((n_tiles * tmE, D), F32),
        compiler_params=_cp("arbitrary", "arbitrary"),
    )(tile_expert, n_active, src, h, wg, wu, wd)


def _combine_kernel(pos_ref, y_ref, x_ref, gate_ref, gw_ref, o_ref, buf, sem, *, tmc):
    base = pl.program_id(0) * tmc
    G, R, D = x_ref.shape

    def issue(r, c):
        for s in range(2):
            _row_copy(y_ref, pos_ref[2 * (base + r) + s], buf.at[s], r, sem).start()
        return c

    lax.fori_loop(0, tmc, issue, 0)
    for s in range(2):
        pltpu.make_async_copy(y_ref.at[pl.ds(0, tmc), :], buf.at[s], sem).wait()
    gw = gw_ref[...]
    y = gw[:, 0:1] * buf[0] + gw[:, 1:2] * buf[1]
    o_ref[...] = x_ref[...] + gate_ref[...] * y.reshape(G, R, D)


def _combine(pos, y, x3, gate, gw, *, tmc):
    NG, R, D = x3.shape
    G = tmc // R
    return pl.pallas_call(
        functools.partial(_combine_kernel, tmc=tmc),
        grid_spec=pltpu.PrefetchScalarGridSpec(
            num_scalar_prefetch=1,
            grid=(NG // G,),
            in_specs=[pl.BlockSpec(memory_space=pl.ANY),
                      pl.BlockSpec((G, R, D), lambda i, p: (i, 0, 0)),
                      pl.BlockSpec((G, 1, D), lambda i, p: (i, 0, 0)),
                      pl.BlockSpec((tmc, 128), lambda i, p: (i, 0))],
            out_specs=pl.BlockSpec((G, R, D), lambda i, p: (i, 0, 0)),
            scratch_shapes=[pltpu.VMEM((2, tmc, D), F32), pltpu.SemaphoreType.DMA(())]),
        out_shape=jax.ShapeDtypeStruct(x3.shape, F32),
        compiler_params=_cp("arbitrary"),
    )(pos, y, x3, gate, gw)


def _moe_plan(idx2, *, tmE, n_tiles):
    T = idx2.shape[0]
    flat = idx2.reshape(-1)
    onehot = (flat[:, None] == jnp.arange(N_EXPERTS, dtype=I32)[None, :]).astype(I32)
    csum = jnp.cumsum(onehot, axis=0)
    rank = jnp.sum(csum * onehot, axis=1) - 1
    counts = csum[-1]
    padded = ((counts + tmE - 1) // tmE) * tmE
    ends = jnp.cumsum(padded)
    starts = ends - padded
    pos = jnp.sum(starts[None, :] * onehot, axis=1) + rank
    src = jnp.zeros((n_tiles * tmE,), I32).at[pos].set(jnp.arange(2 * T, dtype=I32) // 2)
    n_active = ends[-1] // tmE
    tile_start = jnp.arange(n_tiles, dtype=I32) * tmE
    te = jnp.sum((tile_start[:, None] >= ends[None, :]).astype(I32), axis=1)
    last = jnp.sum(((n_active - 1) * tmE >= ends).astype(I32))
    te = jnp.where(jnp.arange(n_tiles) < n_active, te, last).astype(I32)
    return pos.astype(I32), src, te, n_active.astype(I32).reshape(1)


def _moe(x3, shift, scale, gate, g, wr, wg, wu, wd, li, *, tm, tmE, tf, tmc, chunk):
    NG, R, D = x3.shape
    T = NG * R
    wr_pad = jnp.zeros((D, 128), F32).at[:, :N_EXPERTS].set(wr)
    wr3 = jnp.stack(_split3(wr_pad))
    h, idx, gw = _router(x3, shift, scale, g, wr3, tm=tm)
    n_tiles = (2 * T) // tmE + N_EXPERTS
    pos, src, te, n_active = _moe_plan(idx[:, :2], tmE=tmE, n_tiles=n_tiles)
    y = _moe_ffn(te, n_active, src, h, wg, wu, wd, li, n_tiles=n_tiles, tmE=tmE, tf=tf, chunk=chunk)
    return _combine(pos, y, x3, gate, gw, tmc=tmc)


def _final_kernel(x_ref, g_ref, o_ref):
    x = x_ref[...]
    ms = jnp.mean(x * x, axis=-1, keepdims=True)
    o_ref[...] = x * lax.rsqrt(ms + EPS) * g_ref[...]


def _final_norm(x2, g, *, tm):
    T, D = x2.shape
    return pl.pallas_call(
        _final_kernel,
        grid=(T // tm,),
        in_specs=[pl.BlockSpec((tm, D), lambda i: (i, 0)), pl.BlockSpec(g.shape, lambda i: (0, 0))],
        out_specs=pl.BlockSpec((tm, D), lambda i: (i, 0)),
        out_shape=jax.ShapeDtypeStruct((T, D), F32),
        compiler_params=_cp("parallel"),
    )(x2, g)


def _segment_sum_matrix():
    r = jnp.arange(GLA_HEADS * GLA_DK)[:, None] // GLA_DK
    c = jnp.arange(GLA_HEADS * GLA_DV)[None, :] // GLA_DV
    return (r == c).astype(BF16)


def _bias_placement():
    place = jnp.zeros((384, 256), F32)
    one = jnp.zeros((1, 256), F32)
    for h in range(FOX_HEADS):
        for part in range(3):
            place = place.at[128 * part + FF_LANE + h, 16 * h + part].set(1.0)
            place = place.at[128 * part + FF_LANE + h, 128 + 16 * h + 3 + part].set(-1.0)
            one = one.at[0, 16 * h + 3 + part].set(1.0)
            one = one.at[0, 128 + 16 * h + part].set(1.0)
    return place.astype(BF16), one


def _pack_in_weights(w_in_l, b_f_l, w_a2_l, b_a_l):
    D = w_in_l.shape[0]
    o_ff, o_g = 1536, 1544
    o_ga = o_g + 256 + 256 + 512 + 512
    w = jnp.concatenate([w_in_l[:, :o_ff], w_in_l[:, o_g:o_ga]], axis=1).astype(BF16)
    wvt = w_in_l[:, 1024:1536].T.astype(BF16)
    wm = jnp.zeros((D, 128), F32).at[:, :GLA_RANK].set(w_in_l[:, o_ga:o_ga + GLA_RANK])
    wm = wm.at[:, FF_LANE:FF_LANE + FOX_HEADS].set(w_in_l[:, o_ff:o_g]).astype(BF16)
    bm = jnp.zeros((1, 128), F32).at[0, FF_LANE:FF_LANE + FOX_HEADS].set(b_f_l)
    wa = jnp.zeros((128, 256), F32).at[:GLA_RANK].set(w_a2_l).astype(BF16)
    return w, wvt, wm, bm, wa, b_a_l.reshape(1, -1)


def kernel(x_prompt, x_sample, cache_fox_k, cache_fox_v, cache_fox_lf, state_gla, c_prompt, c_sample, w_ada, b_ada, g_attn, g_ffn, w_in, b_fox_f, w_gla_a2, b_gla_a, g_gla, w_o, w_ffn_gate, w_ffn_up, w_ffn_down, w_router, w_moe_gate, w_moe_up, w_moe_down, g_final):
    B, S, D = x_prompt.shape
    NB, TN, _ = x_sample.shape
    L = w_in.shape[0]
    P = cache_fox_k.shape[2]
    TP, TS = B * S, NB * TN
    T = TP + TS
    NG = T // GROUP
    W = FOX_HEADS * FOX_DH
    KW = GLA_HEADS * GLA_DK
    tm = 512
    gla_nc = 4

    x3 = jnp.concatenate([x_prompt.reshape(TP, D), x_sample.reshape(TS, D)], axis=0).reshape(NG, GROUP, D)

    n_c = B + NB
    c_all = jnp.zeros((-(-n_c // 8) * 8, D), F32).at[:n_c].set(jnp.concatenate([c_prompt, c_sample], axis=0))
    mods = _ada(c_all, w_ada, b_ada)[:, :n_c].reshape(L, n_c, 6, D)
    mods = jnp.concatenate([jnp.repeat(mods[:, :B], S // GROUP, axis=1), mods[:, B:]], axis=1)

    segb = _segment_sum_matrix()
    place, one = _bias_placement()
    kc_all = cache_fox_k.reshape(L, NB, P, W)
    vc_all = cache_fox_v.reshape(L, NB, P, W)
    lfc_all = jnp.swapaxes(cache_fox_lf, 2, 3)
    s_prompt0 = jnp.zeros((B, KW, GLA_DV), F32)

    outs = {n: [] for n in ("kp", "vp", "lp", "sp", "kn", "vn", "ln", "sn")}
    for l in range(L):
        m = [mods[l, :, j][:, None, :] for j in range(6)]
        w, wvt, wm, bm, wa, ba = _pack_in_weights(w_in[l], b_fox_f[l], w_gla_a2[l], b_gla_a[l])
        (fq, fk32, fv32, fkb, vt, qx, kx, gq, gk, gv, gr, la, lf) = _inproj(
            x3, m[0], m[1], g_attn[l].reshape(1, D), w, wvt, wm, bm, wa, ba, place, one,
            tm=tm, blocks_per_seq=S // tm)
        lf_tok = lf[:, FF_LANE:FF_LANE + FOX_HEADS]

        fo_p = _fox_prompt(fq, qx, fkb, kx, vt, B=B, S=S, tq=512, tk=512)
        lfn3 = lf_tok[TP:].reshape(NB, TN, FOX_HEADS).transpose(0, 2, 1)
        fo_s = _fox_sample(fq[TP:].reshape(NB, TN, W), fkb[TP:].reshape(NB, TN, W),
                           fv32[TP:].astype(BF16).reshape(NB, TN, W),
                           lfn3, kc_all[l], vc_all[l], lfc_all[l], tk=512)

        go_p, s_p = _gla(gq, gk, gv, la, s_prompt0, segb, C=64, NC=gla_nc, chain=True, nseq=B,
                         nsteps=S // (64 * gla_nc), row_block_off=0)
        go_s, s_n = _gla(gq, gk, gv, la, state_gla[l].reshape(NB, KW, GLA_DV), segb, C=TN, NC=gla_nc, chain=False,
                         nseq=NB // gla_nc, nsteps=1, row_block_off=TP // (TN * gla_nc))

        x3 = _outproj(fo_p, fo_s, go_p, go_s, gr, x3, m[2], g_gla[l].reshape(1, GLA_DV), w_o[l].astype(BF16), tm=tm)

        if l % 2 == 0:
            x3 = _ffn(x3, m[3], m[4], m[5], g_ffn[l].reshape(1, D), w_ffn_gate, w_ffn_up, w_ffn_down, l // 2,
                      tm=1024, tf=256)
        else:
            x3 = _moe(x3, m[3], m[4], m[5], g_ffn[l].reshape(1, D), w_router[l // 2], w_moe_gate, w_moe_up,
                      w_moe_down, l // 2, tm=tm, tmE=1024, tf=256, tmc=256, chunk=128)

        outs["kp"].append(fk32[:TP].reshape(B, S, FOX_HEADS, FOX_DH))
        outs["vp"].append(fv32[:TP].reshape(B, S, FOX_HEADS, FOX_DH))
        outs["lp"].append(lf_tok[:TP].reshape(B, S, FOX_HEADS))
        outs["sp"].append(s_p.reshape(B, GLA_HEADS, GLA_DK, GLA_DV))
        outs["kn"].append(fk32[TP:].reshape(NB, TN, FOX_HEADS, FOX_DH))
        outs["vn"].append(fv32[TP:].reshape(NB, TN, FOX_HEADS, FOX_DH))
        outs["ln"].append(lf_tok[TP:].reshape(NB, TN, FOX_HEADS))
        outs["sn"].append(s_n.reshape(NB, GLA_HEADS, GLA_DK, GLA_DV))

    y = _final_norm(x3.reshape(T, D), g_final.reshape(1, D), tm=tm)
    st = lambda n: jnp.stack(outs[n])
    return (y[:TP].reshape(B, S, D), y[TP:].reshape(NB, TN, D),
            st("kp"), st("vp"), st("lp"), st("sp"), st("kn"), st("vn"), st("ln"), st("sn"))
```

```python
import functools
import math

import jax
import jax.numpy as jnp
from jax import lax
from jax.experimental import pallas as pl
from jax.experimental.pallas import tpu as pltpu

F32, BF16, I32 = jnp.float32, jnp.bfloat16, jnp.int32
EPS = 1e-6
NEG = -1e30
LOG2E = math.log2(math.e)
GROUP = 32
FOX_HEADS, FOX_DH = 8, 64
GLA_HEADS, GLA_DK, GLA_DV = 4, 64, 128
GLA_RANK = 16
GLA_TAU = 16.0
N_EXPERTS = 8
SUB = 16
FF_LANE = 16
VMEM_LIMIT = 56 * 1024 * 1024


def _cp(*sem):
    return pltpu.CompilerParams(dimension_semantics=sem, vmem_limit_bytes=VMEM_LIMIT)


def _dot(a, b):
    return jnp.dot(a, b, preferred_element_type=F32)


def _dot_nt(a, b):
    return lax.dot_general(a, b, (((1,), (1,)), ((), ())), preferred_element_type=F32)


def _dot_tn(a, b):
    return lax.dot_general(a, b, (((0,), (0,)), ((), ())), preferred_element_type=F32)


def _split3(x):
    h = x.astype(BF16)
    r = x - h.astype(F32)
    m = r.astype(BF16)
    l = (r - m.astype(F32)).astype(BF16)
    return h, m, l


def _dot3(x, u):
    h, m, l = _split3(x)
    return _dot(h, u) + _dot(m, u) + _dot(l, u)


def _dot3_l(u, x):
    h, m, l = _split3(x)
    return _dot(u, h) + _dot(u, m) + _dot(u, l)


def _dot3_tn(x, u):
    h, m, l = _split3(x)
    return _dot_tn(h, u) + _dot_tn(m, u) + _dot_tn(l, u)


def _silu(x):
    return x * (1.0 / (1.0 + jnp.exp(-x)))


def _log_sigmoid(x):
    return jnp.minimum(x, 0.0) - jnp.log1p(jnp.exp(-jnp.abs(x)))


def _norm_mod(x3, g, shift, scale):
    ms = jnp.mean(x3 * x3, axis=-1, keepdims=True)
    return (x3 * lax.rsqrt(ms + EPS) * g) * (1.0 + scale) + shift


def _iota(shape, axis):
    return lax.broadcasted_iota(I32, shape, axis)


def _ones_where(cond):
    return jnp.where(cond, 1.0, 0.0).astype(BF16)


def _ada_kernel(c_ref, w_ref, b_ref, o_ref):
    a = _silu(c_ref[...]).astype(BF16)
    o_ref[0] = _dot(a, w_ref[0].astype(BF16)) + b_ref[0]


def _ada(c_all, w_ada, b_ada):
    L, D, N = w_ada.shape
    R = c_all.shape[0]
    tn = 1536
    return pl.pallas_call(
        _ada_kernel,
        grid=(L, N // tn),
        in_specs=[pl.BlockSpec((R, D), lambda l, j: (0, 0)),
                  pl.BlockSpec((1, D, tn), lambda l, j: (l, 0, j)),
                  pl.BlockSpec((1, 1, tn), lambda l, j: (l, 0, j))],
        out_specs=pl.BlockSpec((1, R, tn), lambda l, j: (l, 0, j)),
        out_shape=jax.ShapeDtypeStruct((L, R, N), F32),
        compiler_params=_cp("parallel", "parallel"),
    )(c_all, w_ada, b_ada.reshape(L, 1, N))


def _inproj_kernel(x_ref, sh_ref, sc_ref, g_ref, w_ref, wvt_ref, wm_ref, bm_ref, wa_ref, ba_ref, place_ref, one_ref,
                   fq_ref, fk32_ref, fv32_ref, fkb_ref, vt_ref, qx_ref, kx_ref, gq_ref, gk_ref, gv_ref, gr_ref,
                   la_ref, lf_ref, carry_ref, *, blocks_per_seq):
    i = pl.program_id(0)
    G, R, D = x_ref.shape
    tm = G * R
    h = _norm_mod(x_ref[...], g_ref[...], sh_ref[...], sc_ref[...]).reshape(tm, D).astype(BF16)

    fq_ref[...] = (_dot(h, w_ref[:, 0:512]) * (LOG2E * FOX_DH ** -0.5)).astype(BF16)
    z = _dot(h, w_ref[:, 512:1024])
    fk32_ref[...] = z
    fkb_ref[...] = z.astype(BF16)
    fv32_ref[...] = _dot(h, w_ref[:, 1024:1536])
    vt_ref[...] = _dot_nt(wvt_ref[...], h).astype(BF16)
    gq_ref[...] = _dot(h, w_ref[:, 1536:1792]) * (GLA_DK ** -0.5)
    gk_ref[...] = _dot(h, w_ref[:, 1792:2048])
    gv_ref[...] = _dot(h, w_ref[:, 2048:2560]).astype(BF16)
    gr_ref[...] = _dot(h, w_ref[:, 2560:3072])

    zm = _dot(h, wm_ref[...])
    la_pre = _dot(zm.astype(BF16), wa_ref[...]) + ba_ref[...]
    la_ref[...] = _log_sigmoid(la_pre) * (1.0 / GLA_TAU)
    lf = _log_sigmoid(zm + bm_ref[...])
    lf_ref[...] = lf

    @pl.when(i % blocks_per_seq == 0)
    def _():
        carry_ref[...] = jnp.zeros_like(carry_ref)

    half = tm // 2
    ltri = _ones_where(_iota((half, half), 1) <= _iota((half, half), 0))
    parts = jnp.concatenate(_split3(lf), axis=1)

    def fold(c):
        return c[:, 0:128] + c[:, 128:256] + c[:, 256:384]

    f0 = fold(_dot(ltri, parts[:half])) + carry_ref[0:1, :]
    f1 = fold(_dot(ltri, parts[half:])) + f0[half - 1:half, :]
    carry_ref[...] = jnp.broadcast_to(f1[half - 1:half, :], carry_ref.shape)
    f = jnp.concatenate([f0, f1], axis=0) * LOG2E

    xk = _dot(jnp.concatenate(_split3(f), axis=1), place_ref[...]) + one_ref[...]
    qx_ref[...] = xk[:, 0:128].astype(BF16)
    kx_ref[...] = xk[:, 128:256].astype(BF16)


def _inproj(x3, shift, scale, g, w, wvt, wm, bm, wa, ba, place, one, *, tm, blocks_per_seq):
    NG, R, D = x3.shape
    T = NG * R
    G = tm // R
    nb = T // tm
    row = lambda n: pl.BlockSpec((tm, n), lambda i: (i, 0))
    full = lambda a: pl.BlockSpec(a.shape, lambda i: (0,) * a.ndim)
    outs = [
        (jax.ShapeDtypeStruct((T, 512), BF16), row(512)),
        (jax.ShapeDtypeStruct((T, 512), F32), row(512)),
        (jax.ShapeDtypeStruct((T, 512), F32), row(512)),
        (jax.ShapeDtypeStruct((T, 512), BF16), row(512)),
        (jax.ShapeDtypeStruct((512, T), BF16), pl.BlockSpec((512, tm), lambda i: (0, i))),
        (jax.ShapeDtypeStruct((T, 128), BF16), row(128)),
        (jax.ShapeDtypeStruct((T, 128), BF16), row(128)),
        (jax.ShapeDtypeStruct((T, 256), F32), row(256)),
        (jax.ShapeDtypeStruct((T, 256), F32), row(256)),
        (jax.ShapeDtypeStruct((T, 512), BF16), row(512)),
        (jax.ShapeDtypeStruct((T, 512), F32), row(512)),
        (jax.ShapeDtypeStruct((T, 256), F32), row(256)),
        (jax.ShapeDtypeStruct((T, 128), F32), row(128)),
    ]
    return pl.pallas_call(
        functools.partial(_inproj_kernel, blocks_per_seq=blocks_per_seq),
        grid=(nb,),
        in_specs=[pl.BlockSpec((G, R, D), lambda i: (i, 0, 0)),
                  pl.BlockSpec((G, 1, D), lambda i: (i, 0, 0)),
                  pl.BlockSpec((G, 1, D), lambda i: (i, 0, 0)),
                  full(g), full(w), full(wvt), full(wm), full(bm), full(wa), full(ba), full(place), full(one)],
        out_specs=[o[1] for o in outs],
        out_shape=[o[0] for o in outs],
        scratch_shapes=[pltpu.VMEM((8, 128), F32)],
        compiler_params=_cp("arbitrary"),
    )(x3, shift, scale, g, w, wvt, wm, bm, wa, ba, place, one)


def _fox_prompt_kernel(q_ref, qx_ref, k_ref, kx_ref, vt_ref, o_ref, sa_scr, sb_scr, m_scr, l_scr, acc_scr, *, t):
    p = pl.program_id(1)
    i = pl.program_id(2)
    lane = _iota((1, 128), 1)
    q2 = q_ref[...]
    qx = qx_ref[...]
    zero = jnp.zeros_like(q2)
    qa = []
    for hh in range(2):
        mine = (lane < FOX_DH) if hh == 0 else (lane >= FOX_DH)
        qa.append(jnp.concatenate([jnp.where(mine, q2, zero),
                                   jnp.where(lane // 16 == 2 * p + hh, qx, zero)], axis=1))
    m_scr[...] = jnp.full(m_scr.shape, NEG, F32)
    l_scr[...] = jnp.zeros_like(l_scr)
    acc_scr[...] = jnp.zeros_like(acc_scr)

    def key_start(kb):
        return pl.multiple_of(kb * t, t)

    def scores(kb):
        k0 = key_start(kb)
        ka = jnp.concatenate([k_ref[pl.ds(k0, t), :], kx_ref[pl.ds(k0, t), :]], axis=1)
        return [_dot_nt(ka, qa[hh]) for hh in range(2)]

    def scores_into(kb, buf):
        st = scores(kb)
        for hh in range(2):
            buf[hh] = st[hh]

    def consume(kb, st):
        k0 = key_start(kb)
        for hh in range(2):
            m_old = m_scr[hh]
            m_new = jnp.maximum(m_old, jnp.max(st[hh], axis=0, keepdims=True))
            alpha = jnp.exp2(m_old - m_new)
            pt = jnp.exp2(st[hh] - m_new)
            l_scr[hh] = alpha * l_scr[hh] + jnp.sum(pt, axis=0, keepdims=True)
            m_scr[hh] = m_new
            rows = slice(FOX_DH * hh, FOX_DH * (hh + 1))
            acc_scr[rows, :] = acc_scr[rows, :] * alpha + _dot(vt_ref[rows, pl.ds(k0, t)], pt.astype(BF16))

    def consume_from(kb, buf):
        consume(kb, [buf[hh] for hh in range(2)])

    @pl.when(i >= 1)
    def _():
        scores_into(0, sa_scr)

    def pair(kk, c):
        kb = 2 * kk
        scores_into(kb + 1, sb_scr)
        consume_from(kb, sa_scr)
        scores_into(kb + 2, sa_scr)
        consume_from(kb + 1, sb_scr)
        return c

    lax.fori_loop(0, (i - 1) // 2, pair, 0)

    @pl.when(jnp.logical_and(i >= 1, i % 2 == 0))
    def _():
        scores_into(i - 1, sb_scr)
        consume_from(i - 2, sa_scr)
        consume_from(i - 1, sb_scr)

    @pl.when(i % 2 == 1)
    def _():
        consume_from(i - 1, sa_scr)

    st = scores(i)
    visible = _iota((t, t), 0) <= _iota((t, t), 1)
    consume(i, [jnp.where(visible, s_, NEG) for s_ in st])
    inv = jnp.concatenate([jnp.broadcast_to(1.0 / l_scr[hh], (FOX_DH, t)) for hh in range(2)], axis=0)
    o_ref[...] = jnp.transpose(acc_scr[...] * inv).astype(o_ref.dtype)


def _fox_prompt(fq, qx, fkb, kx, vt, *, B, S, t):
    nq = S // t
    tq = t
    return pl.pallas_call(
        functools.partial(_fox_prompt_kernel, t=t),
        grid=(B, FOX_HEADS // 2, nq),
        in_specs=[pl.BlockSpec((tq, 128), lambda b, p, i: (b * nq + i, p)),
                  pl.BlockSpec((tq, 128), lambda b, p, i: (b * nq + i, 0)),
                  pl.BlockSpec((S, 128), lambda b, p, i: (b, p)),
                  pl.BlockSpec((S, 128), lambda b, p, i: (b, 0)),
                  pl.BlockSpec((128, S), lambda b, p, i: (p, b))],
        out_specs=pl.BlockSpec((tq, 128), lambda b, p, i: (b * nq + i, p)),
        out_shape=jax.ShapeDtypeStruct((B * S, 512), BF16),
        scratch_shapes=[pltpu.VMEM((2, t, t), F32), pltpu.VMEM((2, t, t), F32),
                        pltpu.VMEM((2, 1, t), F32), pltpu.VMEM((2, 1, t), F32),
                        pltpu.VMEM((128, t), F32)],
        compiler_params=_cp("parallel", "parallel", "arbitrary"),
    )(fq, qx, fkb, kx, vt)


def _suffix_sums(x, carry):
    n = x.shape[1]
    nch = n // 256
    xs = jnp.concatenate([x[:, c * 256:(c + 1) * 256] for c in range(nch)], axis=0)
    later = _ones_where(_iota((256, 256), 0) > _iota((256, 256), 1))
    r = _dot(jnp.concatenate(_split3(xs), axis=0), later)
    rows = 8 * nch
    w = r[0:rows] + r[rows:2 * rows] + r[2 * rows:3 * rows]
    tot = w[:, 0:1] + xs[:, 0:1]
    pieces = []
    run = carry
    for c in reversed(range(nch)):
        pieces.append(w[8 * c:8 * (c + 1)] + run)
        run = run + tot[8 * c:8 * (c + 1)]
    pieces.reverse()
    return jnp.concatenate(pieces, axis=1), run


def _fox_sample_kernel(q_ref, kn_ref, vn_ref, lfn_ref, kt_ref, vt_ref, lfc_ref, o_ref,
                       qbd_scr, fn_scr, m_scr, l_scr, acc_scr, g_scr, *, tk, nkb):
    kb = pl.program_id(1)
    TN = q_ref.shape[1]
    HT = FOX_HEADS * TN
    W = FOX_HEADS * FOX_DH

    def online(s, pv):
        m_old = m_scr[...]
        m_new = jnp.maximum(m_old, jnp.max(s, axis=1, keepdims=True))
        alpha = jnp.exp2(m_old - m_new)
        p = jnp.exp2(s - m_new)
        l_scr[...] = alpha * l_scr[...] + jnp.sum(p, axis=1, keepdims=True)
        m_scr[...] = m_new
        acc_scr[...] = acc_scr[...] * alpha + pv(p.astype(BF16))

    @pl.when(kb == 0)
    def _():
        q = q_ref[0]
        qt = jnp.concatenate([q] * FOX_HEADS, axis=0)
        same = (_iota((HT, W), 0) // TN) == (_iota((HT, W), 1) // FOX_DH)
        qbd = jnp.where(same, qt, jnp.zeros_like(qt))
        qbd_scr[...] = qbd
        tri = _ones_where(_iota((TN, TN), 0) <= _iota((TN, TN), 1))
        fnt = _dot3(lfn_ref[0] * LOG2E, tri)
        rep = jnp.broadcast_to(fnt[:, None, :], (FOX_HEADS, TN, TN)).reshape(HT, TN)
        tpos = _iota((HT, TN), 0) % TN
        jpos = _iota((HT, TN), 1)
        fn_col = jnp.sum(jnp.where(tpos == jpos, rep, 0.0), axis=1, keepdims=True)
        fn_scr[...] = fn_col
        m_scr[...] = jnp.full(m_scr.shape, NEG, F32)
        l_scr[...] = jnp.zeros_like(l_scr)
        acc_scr[...] = jnp.zeros_like(acc_scr)
        g_scr[...] = jnp.zeros_like(g_scr)
        s = _dot_nt(qbd, kn_ref[0]) + fn_col - rep
        online(jnp.where(jpos <= tpos, s, NEG), lambda p: _dot(p, vn_ref[0]))

    gl, total = _suffix_sums(lfc_ref[0, 0] * LOG2E, g_scr[:, 0:1])
    g_scr[...] = jnp.broadcast_to(total, g_scr.shape)
    grep = jnp.broadcast_to(gl[:, None, :], (FOX_HEADS, TN, tk)).reshape(HT, tk)
    s = _dot(qbd_scr[...], kt_ref[0, 0].astype(BF16)) + fn_scr[...] + grep
    online(s, lambda p: _dot_nt(p, vt_ref[0, 0].astype(BF16)))

    @pl.when(kb == nkb - 1)
    def _():
        accn = acc_scr[...] * (1.0 / l_scr[...])
        lane_head = _iota((TN, W), 1) // FOX_DH
        out = jnp.zeros((TN, W), F32)
        for h in range(FOX_HEADS):
            out = out + jnp.where(lane_head == h, accn[h * TN:(h + 1) * TN, :], 0.0)
        o_ref[...] = out.astype(o_ref.dtype)


def _fox_sample(q3, kn3, vn3, lfn3, kc_t, vc_t, lfc_t, layer, *, tk):
    NB, TN, W = q3.shape
    P = kc_t.shape[3]
    nkb = P // tk
    HT = FOX_HEADS * TN
    new = lambda: pl.BlockSpec((1, TN, W), lambda b, k: (b, 0, 0))
    return pl.pallas_call(
        functools.partial(_fox_sample_kernel, tk=tk, nkb=nkb),
        grid=(NB, nkb),
        in_specs=[new(), new(), new(),
                  pl.BlockSpec((1, FOX_HEADS, TN), lambda b, k: (b, 0, 0)),
                  pl.BlockSpec((1, 1, W, tk), lambda b, k: (layer, b, 0, nkb - 1 - k)),
                  pl.BlockSpec((1, 1, W, tk), lambda b, k: (layer, b, 0, nkb - 1 - k)),
                  pl.BlockSpec((1, 1, FOX_HEADS, tk), lambda b, k: (layer, b, 0, nkb - 1 - k))],
        out_specs=pl.BlockSpec((TN, W), lambda b, k: (b, 0)),
        out_shape=jax.ShapeDtypeStruct((NB * TN, W), BF16),
        scratch_shapes=[pltpu.VMEM((HT, W), BF16), pltpu.VMEM((HT, 1), F32), pltpu.VMEM((HT, 1), F32),
                        pltpu.VMEM((HT, 1), F32), pltpu.VMEM((HT, W), F32), pltpu.VMEM((FOX_HEADS, 128), F32)],
        compiler_params=_cp("parallel", "arbitrary"),
    )(q3, kn3, vn3, lfn3, kc_t, vc_t, lfc_t)


def _gla_intra(q, k, v, la, segb, tri, ones_cv):
    C = q.shape[0]
    KW, VW = GLA_HEADS * GLA_DK, GLA_HEADS * GLA_DV
    b = _dot3_l(tri, la)
    blast = b[C - 1:C, :]
    qe = (q * jnp.exp(b)).astype(BF16)
    kd = (k * jnp.exp(blast - b)).astype(BF16)
    blocks = []
    for blk in range(C // SUB):
        r0 = blk * SUB
        bi, qi, ki = b[r0:r0 + SUB], q[r0:r0 + SUB], k[r0:r0 + SUB]
        diff = bi[None, :, :] - bi[:, None, :]
        causal = _iota((SUB, SUB, 1), 1) >= _iota((SUB, SUB, 1), 0)
        tm3 = jnp.exp(jnp.where(causal, diff, -jnp.inf)) * qi[None, :, :] * ki[:, None, :]
        arep = _dot(tm3.reshape(SUB * SUB, KW).astype(BF16), segb)
        vi = v[r0:r0 + SUB].astype(F32)
        o_blk = jnp.sum(arep.reshape(SUB, SUB, VW) * vi[:, None, :], axis=0)
        if blk > 0:
            bref = b[r0 - 1:r0]
            qs = (qi * jnp.exp(bi - bref)).astype(BF16)
            ks = (k[:r0] * jnp.exp(bref - b[:r0])).astype(BF16)
            parts = []
            for h in range(GLA_HEADS):
                ks_, vs_ = slice(h * GLA_DK, (h + 1) * GLA_DK), slice(h * GLA_DV, (h + 1) * GLA_DV)
                a = _dot_nt(qs[:, ks_], ks[:, ks_])
                parts.append(_dot(a.astype(BF16), v[:r0, vs_]))
            o_blk = o_blk + jnp.concatenate(parts, axis=1)
        blocks.append(o_blk)
    o_intra = jnp.concatenate(blocks, axis=0)
    dec = jnp.exp(_dot3_tn(la, ones_cv))
    upd = _dot_tn(kd, v)
    upd = jnp.concatenate([upd[h * GLA_DK:(h + 1) * GLA_DK, h * GLA_DV:(h + 1) * GLA_DV]
                           for h in range(GLA_HEADS)], axis=0)
    return qe, o_intra, dec, upd


def _gla_kernel(q_ref, k_ref, v_ref, la_ref, s0_ref, segb_ref, o_ref, sfin_ref, *scr, C, NC, chain, nsteps):
    if chain:
        (s_scr,) = scr

        @pl.when(pl.program_id(1) == 0)
        def _():
            s_scr[...] = s0_ref[0]

    tri = _ones_where(_iota((C, C), 1) <= _iota((C, C), 0))
    ones_cv = jnp.ones((C, GLA_DV), BF16)
    segb = segb_ref[...]
    units = []
    for u in range(NC):
        rows = slice(u * C, (u + 1) * C)
        units.append(_gla_intra(q_ref[rows, :], k_ref[rows, :], v_ref[rows, :], la_ref[rows, :], segb, tri, ones_cv))

    state = s_scr[...] if chain else None
    for u, (qe, o_intra, dec, upd) in enumerate(units):
        if not chain:
            state = s0_ref[u]
        s_bf = state.astype(BF16)
        o_inter = jnp.concatenate(
            [_dot(qe[:, h * GLA_DK:(h + 1) * GLA_DK], s_bf[h * GLA_DK:(h + 1) * GLA_DK, :]) for h in range(GLA_HEADS)],
            axis=1)
        o_ref[u * C:(u + 1) * C, :] = o_intra + o_inter
        state = dec * state + upd
        if not chain:
            sfin_ref[u] = state
    if chain:
        s_scr[...] = state

        @pl.when(pl.program_id(1) == nsteps - 1)
        def _():
            sfin_ref[0] = state


def _gla(gq, gk, gv, la, s0, segb, *, C, NC, chain, nseq, nsteps, row_block_off):
    KW, VW = GLA_HEADS * GLA_DK, GLA_HEADS * GLA_DV
    rows = NC * C
    ns = 1 if chain else NC
    tok = lambda n: pl.BlockSpec((rows, n), lambda s, c: (row_block_off + s * nsteps + c, 0))
    return pl.pallas_call(
        functools.partial(_gla_kernel, C=C, NC=NC, chain=chain, nsteps=nsteps),
        grid=(nseq, nsteps),
        in_specs=[tok(KW), tok(KW), tok(VW), tok(KW),
                  pl.BlockSpec((ns, KW, GLA_DV), lambda s, c: (s, 0, 0)),
                  pl.BlockSpec(segb.shape, lambda s, c: (0, 0))],
        out_specs=[pl.BlockSpec((rows, VW), lambda s, c: (s * nsteps + c, 0)),
                   pl.BlockSpec((ns, KW, GLA_DV), lambda s, c: (s, 0, 0))],
        out_shape=[jax.ShapeDtypeStruct((nseq * nsteps * rows, VW), F32),
                   jax.ShapeDtypeStruct((nseq * ns, KW, GLA_DV), F32)],
        scratch_shapes=[pltpu.VMEM((KW, GLA_DV), F32)] if chain else [],
        compiler_params=_cp("parallel", "arbitrary"),
    )(gq, gk, gv, la, s0, segb)


def _outproj_kernel(fop_ref, fos_ref, gop_ref, gos_ref, gr_ref, x_ref, gate_ref, gg_ref, wo_ref, o_ref, *, nbp):
    G, R, D = x_ref.shape
    is_prompt = pl.program_id(0) < nbp
    fo = jnp.where(is_prompt, fop_ref[...], fos_ref[...])
    go = jnp.where(is_prompt, gop_ref[...], gos_ref[...])
    parts = []
    for h in range(GLA_HEADS):
        seg = go[:, h * GLA_DV:(h + 1) * GLA_DV]
        ms = jnp.mean(seg * seg, axis=-1, keepdims=True)
        parts.append(seg * lax.rsqrt(ms + EPS) * gg_ref[...])
    gn = jnp.concatenate(parts, axis=1) * _silu(gr_ref[...])
    mix = _dot(fo, wo_ref[0:512, :]) + _dot(gn.astype(BF16), wo_ref[512:1024, :])
    o_ref[...] = x_ref[...] + gate_ref[...] * mix.reshape(G, R, D)


def _outproj(fo_p, fo_s, go_p, go_s, gr, x3, gate, gg, wo, *, tm):
    NG, R, D = x3.shape
    G = tm // R
    nbp = fo_p.shape[0] // tm
    row = lambda n: pl.BlockSpec((tm, n), lambda i: (i, 0))
    prow = lambda n: pl.BlockSpec((tm, n), lambda i: (jnp.minimum(i, nbp - 1), 0))
    srow = lambda n: pl.BlockSpec((tm, n), lambda i: (jnp.maximum(i - nbp, 0), 0))
    return pl.pallas_call(
        functools.partial(_outproj_kernel, nbp=nbp),
        grid=(NG // G,),
        in_specs=[prow(512), srow(512), prow(512), srow(512), row(512),
                  pl.BlockSpec((G, R, D), lambda i: (i, 0, 0)),
                  pl.BlockSpec((G, 1, D), lambda i: (i, 0, 0)),
                  pl.BlockSpec(gg.shape, lambda i: (0, 0)),
                  pl.BlockSpec(wo.shape, lambda i: (0, 0))],
        out_specs=pl.BlockSpec((G, R, D), lambda i: (i, 0, 0)),
        out_shape=jax.ShapeDtypeStruct(x3.shape, F32),
        compiler_params=_cp("parallel"),
    )(fo_p, fo_s, go_p, go_s, gr, x3, gate, gg, wo)


def _ffn_kernel(x_ref, sh_ref, sc_ref, gate_ref, g_ref, wg_ref, wu_ref, wd_ref, o_ref, h_scr, acc_scr, *, nf):
    j = pl.program_id(1)
    G, R, D = x_ref.shape

    @pl.when(j == 0)
    def _():
        h_scr[...] = _norm_mod(x_ref[...], g_ref[...], sh_ref[...], sc_ref[...]).reshape(G * R, D).astype(BF16)
        acc_scr[...] = jnp.zeros_like(acc_scr)

    h = h_scr[...]
    a = _dot(h, wg_ref[0].astype(BF16))
    u = _dot(h, wu_ref[0].astype(BF16))
    acc_scr[...] += _dot((_silu(a) * u).astype(BF16), wd_ref[0].astype(BF16))

    @pl.when(j == nf - 1)
    def _():
        o_ref[...] = x_ref[...] + gate_ref[...] * acc_scr[...].reshape(G, R, D)


def _ffn(x3, shift, scale, gate, g, wg, wu, wd, li, *, tm, tf):
    NG, R, D = x3.shape
    G = tm // R
    F = wg.shape[-1]
    nf = F // tf
    grp = lambda n: pl.BlockSpec((G, n, D), lambda i, j: (i, 0, 0))
    return pl.pallas_call(
        functools.partial(_ffn_kernel, nf=nf),
        grid=(NG // G, nf),
        in_specs=[grp(R), grp(1), grp(1), grp(1),
                  pl.BlockSpec(g.shape, lambda i, j: (0, 0)),
                  pl.BlockSpec((1, D, tf), lambda i, j: (li, 0, j)),
                  pl.BlockSpec((1, D, tf), lambda i, j: (li, 0, j)),
                  pl.BlockSpec((1, tf, D), lambda i, j: (li, j, 0))],
        out_specs=grp(R),
        out_shape=jax.ShapeDtypeStruct(x3.shape, F32),
        scratch_shapes=[pltpu.VMEM((tm, D), BF16), pltpu.VMEM((tm, D), F32)],
        compiler_params=_cp("parallel", "arbitrary"),
    )(x3, shift, scale, gate, g, wg, wu, wd)


def _router_kernel(x_ref, sh_ref, sc_ref, g_ref, wr_ref, h_ref, idx_ref, gw_ref):
    G, R, D = x_ref.shape
    h = _norm_mod(x_ref[...], g_ref[...], sh_ref[...], sc_ref[...]).reshape(G * R, D)
    h_ref[...] = h
    hh, hm, hl = _split3(h)
    wh, wm, wl = wr_ref[0], wr_ref[1], wr_ref[2]
    logits = (_dot(hh, wh) + _dot(hh, wm) + _dot(hm, wh)
              + _dot(hh, wl) + _dot(hm, wm) + _dot(hl, wh))
    lane = _iota(logits.shape, 1)
    logits = jnp.where(lane < N_EXPERTS, logits, -jnp.inf)
    v1 = jnp.max(logits, axis=1, keepdims=True)
    i1 = jnp.min(jnp.where(logits == v1, lane, 128), axis=1, keepdims=True)
    rest = jnp.where(lane == i1, -jnp.inf, logits)
    v2 = jnp.max(rest, axis=1, keepdims=True)
    i2 = jnp.min(jnp.where(rest == v2, lane, 128), axis=1, keepdims=True)
    e = jnp.exp(v2 - v1)
    g1 = 1.0 / (1.0 + e)
    idx_ref[...] = jnp.where(lane == 0, i1, jnp.where(lane == 1, i2, 0))
    gw_ref[...] = jnp.where(lane == 0, g1, jnp.where(lane == 1, e * g1, 0.0))


def _router(x3, shift, scale, g, wr3, *, tm):
    NG, R, D = x3.shape
    T = NG * R
    G = tm // R
    grp = lambda n: pl.BlockSpec((G, n, D), lambda i: (i, 0, 0))
    return pl.pallas_call(
        _router_kernel,
        grid=(NG // G,),
        in_specs=[grp(R), grp(1), grp(1),
                  pl.BlockSpec(g.shape, lambda i: (0, 0)),
                  pl.BlockSpec(wr3.shape, lambda i: (0, 0, 0))],
        out_specs=[pl.BlockSpec((tm, D), lambda i: (i, 0)),
                   pl.BlockSpec((tm, 128), lambda i: (i, 0)),
                   pl.BlockSpec((tm, 128), lambda i: (i, 0))],
        out_shape=[jax.ShapeDtypeStruct((T, D), F32), jax.ShapeDtypeStruct((T, 128), I32),
                   jax.ShapeDtypeStruct((T, 128), F32)],
        compiler_params=_cp("parallel"),
    )(x3, shift, scale, g, wr3)


def _row_copy(src_ref, src_row, dst_ref, dst_row, sem):
    return pltpu.make_async_copy(src_ref.at[pl.ds(src_row, 1), :], dst_ref.at[pl.ds(dst_row, 1), :], sem)


def _moe_ffn_kernel(te_ref, na_ref, src_ref, dst_ref, h_hbm, wg_ref, wu_ref, wd_ref, y_hbm, xbuf, h_scr, acc, gsem, ssem,
                    *, nf, tmE, chunk):
    del te_ref
    i, j = pl.program_id(0), pl.program_id(1)
    na = na_ref[0]
    slot = i % 2
    other = 1 - slot
    r0 = j * chunk

    def send_rows():
        base = i * tmE + r0
        for u in range(chunk):
            _row_copy(acc.at[other], r0 + u, y_hbm, dst_ref[base + u], ssem.at[slot]).start()

    @pl.when(jnp.logical_and(i == 0, j == 0))
    def _():
        def issue(r, c):
            _row_copy(h_hbm, src_ref[r], xbuf.at[0], r, gsem.at[0]).start()
            return c

        lax.fori_loop(0, tmE, issue, 0)
        acc[1] = jnp.zeros((tmE, acc.shape[2]), F32)

    @pl.when(jnp.logical_and(i <= na, j == 0))
    def _():
        pltpu.make_async_copy(h_hbm.at[pl.ds(0, tmE), :], xbuf.at[slot], gsem.at[slot]).wait()

        @pl.when(i >= 1)
        def _():
            pltpu.make_async_copy(acc.at[slot], y_hbm.at[pl.ds(0, tmE), :], ssem.at[other]).wait()

        h_scr[...] = xbuf[slot].astype(BF16)
        acc[slot] = jnp.zeros((tmE, acc.shape[2]), F32)

    @pl.when(i < na)
    def _():
        base = (i + 1) * tmE + r0
        for u in range(chunk):
            _row_copy(h_hbm, src_ref[base + u], xbuf.at[other], r0 + u, gsem.at[other]).start()
        send_rows()
        h = h_scr[...]
        a = _dot(h, wg_ref[0, 0].astype(BF16))
        u = _dot(h, wu_ref[0, 0].astype(BF16))
        acc[slot] += _dot((_silu(a) * u).astype(BF16), wd_ref[0, 0].astype(BF16))

    @pl.when(i == na)
    def _():
        send_rows()

        @pl.when(j == nf - 1)
        def _():
            pltpu.make_async_copy(acc.at[other], y_hbm.at[pl.ds(0, tmE), :], ssem.at[slot]).wait()


def _moe_ffn(tile_expert, n_active, src, dst, h, wg, wu, wd, li, *, n_tiles, tmE, tf, chunk):
    T, D = h.shape
    F = wg.shape[-1]
    nf = F // tf
    assert chunk * nf == tmE
    jj = lambda i, j, na: jnp.where(i < na[0], j, nf - 1)
    w_in = lambda: pl.BlockSpec((1, 1, D, tf), lambda i, j, te, na, sr, ds: (li, te[i], 0, jj(i, j, na)))
    return pl.pallas_call(
        functools.partial(_moe_ffn_kernel, nf=nf, tmE=tmE, chunk=chunk),
        grid_spec=pltpu.PrefetchScalarGridSpec(
            num_scalar_prefetch=4,
            grid=(n_tiles, nf),
            in_specs=[pl.BlockSpec(memory_space=pl.ANY), w_in(), w_in(),
                      pl.BlockSpec((1, 1, tf, D), lambda i, j, te, na, sr, ds: (li, te[i], jj(i, j, na), 0))],
            out_specs=pl.BlockSpec(memory_space=pl.ANY),
            scratch_shapes=[pltpu.VMEM((2, tmE, D), F32), pltpu.VMEM((tmE, D), BF16), pltpu.VMEM((2, tmE, D), F32),
                            pltpu.SemaphoreType.DMA((2,)), pltpu.SemaphoreType.DMA((2,))]),
        out_shape=jax.ShapeDtypeStruct((2 * T + tmE, D), F32),
        compiler_params=_cp("arbitrary", "arbitrary"),
    )(tile_expert, n_active, src, dst, h, wg, wu, wd)


def _combine_kernel(y0_ref, y1_ref, x_ref, gate_ref, gw_ref, o_ref):
    G, R, D = x_ref.shape
    gw = gw_ref[...]
    y = gw[:, 0:1] * y0_ref[...] + gw[:, 1:2] * y1_ref[...]
    o_ref[...] = x_ref[...] + gate_ref[...] * y.reshape(G, R, D)


def _combine(y2, x3, gate, gw, *, tm):
    NG, R, D = x3.shape
    G = tm // R
    nb = NG // G
    return pl.pallas_call(
        _combine_kernel,
        grid=(nb,),
        in_specs=[pl.BlockSpec((tm, D), lambda i: (i, 0)),
                  pl.BlockSpec((tm, D), lambda i: (nb + i, 0)),
                  pl.BlockSpec((G, R, D), lambda i: (i, 0, 0)),
                  pl.BlockSpec((G, 1, D), lambda i: (i, 0, 0)),
                  pl.BlockSpec((tm, 128), lambda i: (i, 0))],
        out_specs=pl.BlockSpec((G, R, D), lambda i: (i, 0, 0)),
        out_shape=jax.ShapeDtypeStruct(x3.shape, F32),
        compiler_params=_cp("parallel"),
    )(y2, y2, x3, gate, gw)


def _moe_plan(idx2, *, tmE, n_tiles):
    T = idx2.shape[0]
    R = n_tiles * tmE
    flat = idx2.reshape(-1)
    onehot = (flat[:, None] == jnp.arange(N_EXPERTS, dtype=I32)[None, :]).astype(I32)
    csum = jnp.cumsum(onehot, axis=0)
    rank = jnp.sum(csum * onehot, axis=1) - 1
    counts = csum[-1]
    padded = ((counts + tmE - 1) // tmE) * tmE
    ends = jnp.cumsum(padded)
    starts = ends - padded
    pos = jnp.sum(starts[None, :] * onehot, axis=1) + rank
    entry = jnp.full((R,), -1, I32).at[pos].set(jnp.arange(2 * T, dtype=I32))
    real = entry >= 0
    spare = 2 * T + jnp.arange(R, dtype=I32) % tmE
    src = jnp.where(real, entry // 2, 0)
    dst = jnp.where(real, (entry % 2) * T + entry // 2, spare)
    dst = jnp.concatenate([spare[:tmE], dst])
    n_active = ends[-1] // tmE
    tile_start = jnp.arange(n_tiles, dtype=I32) * tmE
    te = jnp.sum((tile_start[:, None] >= ends[None, :]).astype(I32), axis=1)
    last = jnp.sum(((n_active - 1) * tmE >= ends).astype(I32))
    te = jnp.where(jnp.arange(n_tiles) < n_active, te, last).astype(I32)
    return te, n_active.astype(I32).reshape(1), src.astype(I32), dst.astype(I32)


def _moe(x3, shift, scale, gate, g, wr, wg, wu, wd, li, *, tm, tmE, tf, chunk):
    NG, R, D = x3.shape
    T = NG * R
    wr_pad = jnp.zeros((D, 128), F32).at[:, :N_EXPERTS].set(wr)
    wr3 = jnp.stack(_split3(wr_pad))
    h, idx, gw = _router(x3, shift, scale, g, wr3, tm=tm)
    n_tiles = -(-(2 * T) // tmE) + N_EXPERTS
    te, n_active, src, dst = _moe_plan(idx[:, :2], tmE=tmE, n_tiles=n_tiles)
    y2 = _moe_ffn(te, n_active, src, dst, h, wg, wu, wd, li, n_tiles=n_tiles, tmE=tmE, tf=tf, chunk=chunk)
    return _combine(y2, x3, gate, gw, tm=tm)


def _final_kernel(x_ref, g_ref, o_ref):
    x = x_ref[...]
    ms = jnp.mean(x * x, axis=-1, keepdims=True)
    o_ref[...] = x * lax.rsqrt(ms + EPS) * g_ref[...]


def _final_norm(x2, g, *, tm):
    T, D = x2.shape
    return pl.pallas_call(
        _final_kernel,
        grid=(T // tm,),
        in_specs=[pl.BlockSpec((tm, D), lambda i: (i, 0)), pl.BlockSpec(g.shape, lambda i: (0, 0))],
        out_specs=pl.BlockSpec((tm, D), lambda i: (i, 0)),
        out_shape=jax.ShapeDtypeStruct((T, D), F32),
        compiler_params=_cp("parallel"),
    )(x2, g)


def _segment_sum_matrix():
    r = jnp.arange(GLA_HEADS * GLA_DK)[:, None] // GLA_DK
    c = jnp.arange(GLA_HEADS * GLA_DV)[None, :] // GLA_DV
    return (r == c).astype(BF16)


def _bias_placement():
    place = jnp.zeros((384, 256), F32)
    one = jnp.zeros((1, 256), F32)
    for h in range(FOX_HEADS):
        for part in range(3):
            place = place.at[128 * part + FF_LANE + h, 16 * h + part].set(1.0)
            place = place.at[128 * part + FF_LANE + h, 128 + 16 * h + 3 + part].set(-1.0)
            one = one.at[0, 16 * h + 3 + part].set(1.0)
            one = one.at[0, 128 + 16 * h + part].set(1.0)
    return place.astype(BF16), one


def _pack_in_weights(w_in_l, b_f_l, w_a2_l, b_a_l):
    D = w_in_l.shape[0]
    o_ff, o_g = 1536, 1544
    o_ga = o_g + 256 + 256 + 512 + 512
    w = jnp.concatenate([w_in_l[:, :o_ff], w_in_l[:, o_g:o_ga]], axis=1).astype(BF16)
    wvt = w_in_l[:, 1024:1536].T.astype(BF16)
    wm = jnp.zeros((D, 128), F32).at[:, :GLA_RANK].set(w_in_l[:, o_ga:o_ga + GLA_RANK])
    wm = wm.at[:, FF_LANE:FF_LANE + FOX_HEADS].set(w_in_l[:, o_ff:o_g]).astype(BF16)
    bm = jnp.zeros((1, 128), F32).at[0, FF_LANE:FF_LANE + FOX_HEADS].set(b_f_l)
    wa = jnp.zeros((128, 256), F32).at[:GLA_RANK].set(w_a2_l).astype(BF16)
    return w, wvt, wm, bm, wa, b_a_l.reshape(1, -1)


def kernel(x_prompt, x_sample, cache_fox_k, cache_fox_v, cache_fox_lf, state_gla, c_prompt, c_sample, w_ada, b_ada, g_attn, g_ffn, w_in, b_fox_f, w_gla_a2, b_gla_a, g_gla, w_o, w_ffn_gate, w_ffn_up, w_ffn_down, w_router, w_moe_gate, w_moe_up, w_moe_down, g_final):
    B, S, D = x_prompt.shape
    NB, TN, _ = x_sample.shape
    L = w_in.shape[0]
    P = cache_fox_k.shape[2]
    TP, TS = B * S, NB * TN
    T = TP + TS
    NG = T // GROUP
    W = FOX_HEADS * FOX_DH
    KW = GLA_HEADS * GLA_DK
    tm = 512
    gla_nc = 4

    x3 = jnp.concatenate([x_prompt.reshape(TP, D), x_sample.reshape(TS, D)], axis=0).reshape(NG, GROUP, D)

    n_c = B + NB
    c_all = jnp.zeros((-(-n_c // 8) * 8, D), F32).at[:n_c].set(jnp.concatenate([c_prompt, c_sample], axis=0))
    mods = _ada(c_all, w_ada, b_ada)[:, :n_c].reshape(L, n_c, 6, D)
    mods = jnp.concatenate([jnp.repeat(mods[:, :B], S // GROUP, axis=1), mods[:, B:]], axis=1)

    segb = _segment_sum_matrix()
    place, one = _bias_placement()
    kc_all = jnp.transpose(cache_fox_k, (0, 1, 3, 4, 2)).reshape(L, NB, W, P)
    vc_all = jnp.transpose(cache_fox_v, (0, 1, 3, 4, 2)).reshape(L, NB, W, P)
    lfc_all = jnp.swapaxes(cache_fox_lf, 2, 3)
    s_prompt0 = jnp.zeros((B, KW, GLA_DV), F32)

    outs = {n: [] for n in ("kp", "vp", "lp", "sp", "kn", "vn", "ln", "sn")}
    for l in range(L):
        m = [mods[l, :, j][:, None, :] for j in range(6)]
        w, wvt, wm, bm, wa, ba = _pack_in_weights(w_in[l], b_fox_f[l], w_gla_a2[l], b_gla_a[l])
        (fq, fk32, fv32, fkb, vt, qx, kx, gq, gk, gv, gr, la, lf) = _inproj(
            x3, m[0], m[1], g_attn[l].reshape(1, D), w, wvt, wm, bm, wa, ba, place, one,
            tm=tm, blocks_per_seq=S // tm)
        lf_tok = lf[:, FF_LANE:FF_LANE + FOX_HEADS]

        fo_p = _fox_prompt(fq, qx, fkb, kx, vt, B=B, S=S, t=512)
        lfn3 = lf_tok[TP:].reshape(NB, TN, FOX_HEADS).transpose(0, 2, 1)
        fo_s = _fox_sample(fq[TP:].reshape(NB, TN, W), fkb[TP:].reshape(NB, TN, W),
                           fv32[TP:].astype(BF16).reshape(NB, TN, W),
                           lfn3, kc_all, vc_all, lfc_all, l, tk=1024)

        go_p, s_p = _gla(gq, gk, gv, la, s_prompt0, segb, C=64, NC=gla_nc, chain=True, nseq=B,
                         nsteps=S // (64 * gla_nc), row_block_off=0)
        go_s, s_n = _gla(gq, gk, gv, la, state_gla[l].reshape(NB, KW, GLA_DV), segb, C=TN, NC=gla_nc, chain=False,
                         nseq=NB // gla_nc, nsteps=1, row_block_off=TP // (TN * gla_nc))

        x3 = _outproj(fo_p, fo_s, go_p, go_s, gr, x3, m[2], g_gla[l].reshape(1, GLA_DV), w_o[l].astype(BF16), tm=tm)

        if l % 2 == 0:
            x3 = _ffn(x3, m[3], m[4], m[5], g_ffn[l].reshape(1, D), w_ffn_gate, w_ffn_up, w_ffn_down, l // 2,
                      tm=1024, tf=256)
        else:
            x3 = _moe(x3, m[3], m[4], m[5], g_ffn[l].reshape(1, D), w_router[l // 2], w_moe_gate, w_moe_up,
                      w_moe_down, l // 2, tm=tm, tmE=1056, tf=256, chunk=96)

        outs["kp"].append(fk32[:TP].reshape(B, S, FOX_HEADS, FOX_DH))
        outs["vp"].append(fv32[:TP].reshape(B, S, FOX_HEADS, FOX_DH))
        outs["lp"].append(lf_tok[:TP].reshape(B, S, FOX_HEADS))
        outs["sp"].append(s_p.reshape(B, GLA_HEADS, GLA_DK, GLA_DV))
        outs["kn"].append(fk32[TP:].reshape(NB, TN, FOX_HEADS, FOX_DH))
        outs["vn"].append(fv32[TP:].reshape(NB, TN, FOX_HEADS, FOX_DH))
        outs["ln"].append(lf_tok[TP:].reshape(NB, TN, FOX_HEADS))
        outs["sn"].append(s_n.reshape(NB, GLA_HEADS, GLA_DK, GLA_DV))

    y = _final_norm(x3.reshape(T, D), g_final.reshape(1, D), tm=tm)
    st = lambda n: jnp.stack(outs[n])
    return (y[:TP].reshape(B, S, D), y[TP:].reshape(NB, TN, D),
            st("kp"), st("vp"), st("lp"), st("sp"), st("kn"), st("vn"), st("ln"), st("sn"))
```

```python
import functools
import math

import jax
import jax.numpy as jnp
from jax import lax
from jax.experimental import pallas as pl
from jax.experimental.pallas import tpu as pltpu

F32, BF16, I32 = jnp.float32, jnp.bfloat16, jnp.int32
EPS = 1e-6
NEG = -1e30
LOG2E = math.log2(math.e)
GROUP = 32
FOX_HEADS, FOX_DH = 8, 64
GLA_HEADS, GLA_DK, GLA_DV = 4, 64, 128
GLA_RANK = 16
GLA_TAU = 16.0
N_EXPERTS = 8
SUB = 16
FF_LANE = 16
VMEM_LIMIT = 56 * 1024 * 1024


def _cp(*sem):
    return pltpu.CompilerParams(dimension_semantics=sem, vmem_limit_bytes=VMEM_LIMIT)


def _dot(a, b):
    return jnp.dot(a, b, preferred_element_type=F32)


def _dot_nt(a, b):
    return lax.dot_general(a, b, (((1,), (1,)), ((), ())), preferred_element_type=F32)


def _dot_tn(a, b):
    return lax.dot_general(a, b, (((0,), (0,)), ((), ())), preferred_element_type=F32)


def _split3(x):
    h = x.astype(BF16)
    r = x - h.astype(F32)
    m = r.astype(BF16)
    l = (r - m.astype(F32)).astype(BF16)
    return h, m, l


def _dot3(x, u):
    h, m, l = _split3(x)
    return _dot(h, u) + _dot(m, u) + _dot(l, u)


def _dot3_l(u, x):
    h, m, l = _split3(x)
    return _dot(u, h) + _dot(u, m) + _dot(u, l)


def _dot3_tn(x, u):
    h, m, l = _split3(x)
    return _dot_tn(h, u) + _dot_tn(m, u) + _dot_tn(l, u)


def _silu(x):
    return x * (1.0 / (1.0 + jnp.exp(-x)))


def _log_sigmoid(x):
    return jnp.minimum(x, 0.0) - jnp.log1p(jnp.exp(-jnp.abs(x)))


def _norm_mod(x3, g, shift, scale):
    ms = jnp.mean(x3 * x3, axis=-1, keepdims=True)
    return (x3 * lax.rsqrt(ms + EPS) * g) * (1.0 + scale) + shift


def _mod(p_ref, s_ref, is_prompt):
    return jnp.where(is_prompt, p_ref[...], s_ref[...])


def _mod_specs(G, D, nbp, bps, grid_rank=1):
    pidx = lambda i: (jnp.minimum(i, nbp - 1) // bps, 0, 0)
    sidx = lambda i: (jnp.maximum(i - nbp, 0), 0, 0)
    if grid_rank == 1:
        return [pl.BlockSpec((1, 1, D), lambda i: pidx(i)), pl.BlockSpec((G, 1, D), lambda i: sidx(i))]
    return [pl.BlockSpec((1, 1, D), lambda i, j: pidx(i)), pl.BlockSpec((G, 1, D), lambda i, j: sidx(i))]


def _iota(shape, axis):
    return lax.broadcasted_iota(I32, shape, axis)


def _ones_where(cond):
    return jnp.where(cond, 1.0, 0.0).astype(BF16)


def _ada_kernel(c_ref, w_ref, b_ref, o_ref):
    a = _silu(c_ref[...]).astype(BF16)
    o_ref[0] = _dot(a, w_ref[0].astype(BF16)) + b_ref[0]


def _ada(c_all, w_ada, b_ada):
    L, D, N = w_ada.shape
    R = c_all.shape[0]
    tn = 1536
    return pl.pallas_call(
        _ada_kernel,
        grid=(L, N // tn),
        in_specs=[pl.BlockSpec((R, D), lambda l, j: (0, 0)),
                  pl.BlockSpec((1, D, tn), lambda l, j: (l, 0, j)),
                  pl.BlockSpec((1, 1, tn), lambda l, j: (l, 0, j))],
        out_specs=pl.BlockSpec((1, R, tn), lambda l, j: (l, 0, j)),
        out_shape=jax.ShapeDtypeStruct((L, R, N), F32),
        compiler_params=_cp("parallel", "parallel"),
    )(c_all, w_ada, b_ada.reshape(L, 1, N))


def _inproj_kernel(x_ref, shp_ref, shs_ref, scp_ref, scs_ref, g_ref, w_ref, wvt_ref, wm_ref, bm_ref, wa_ref, ba_ref,
                   place_ref, one_ref,
                   fq_ref, fkp_ref, fks_ref, fvp_ref, fvs_ref, fkb_ref, vt_ref, qx_ref, kx_ref, gq_ref, gk_ref, gv_ref,
                   gr_ref, la_ref, lf_ref, carry_ref, *, blocks_per_seq, nbp):
    i = pl.program_id(0)
    is_prompt = i < nbp
    G, R, D = x_ref.shape
    tm = G * R
    h = _norm_mod(x_ref[...], g_ref[...], _mod(shp_ref, shs_ref, is_prompt), _mod(scp_ref, scs_ref, is_prompt))
    h = h.reshape(tm, D).astype(BF16)

    fq_ref[...] = (_dot(h, w_ref[:, 0:512]) * (LOG2E * FOX_DH ** -0.5)).astype(BF16)
    zk = _dot(h, w_ref[:, 512:1024])
    fkb_ref[...] = zk.astype(BF16)
    zv = _dot(h, w_ref[:, 1024:1536])

    @pl.when(is_prompt)
    def _():
        fkp_ref[...] = zk
        fvp_ref[...] = zv

    @pl.when(jnp.logical_not(is_prompt))
    def _():
        fks_ref[...] = zk
        fvs_ref[...] = zv

    vt_ref[...] = _dot_nt(wvt_ref[...], h).astype(BF16)
    gq_ref[...] = _dot(h, w_ref[:, 1536:1792]) * (GLA_DK ** -0.5)
    gk_ref[...] = _dot(h, w_ref[:, 1792:2048])
    gv_ref[...] = _dot(h, w_ref[:, 2048:2560]).astype(BF16)
    gr_ref[...] = _dot(h, w_ref[:, 2560:3072])

    zm = _dot(h, wm_ref[...])
    la_pre = _dot(zm.astype(BF16), wa_ref[...]) + ba_ref[...]
    la_ref[...] = _log_sigmoid(la_pre) * (1.0 / GLA_TAU)
    lf = _log_sigmoid(zm + bm_ref[...])
    lf_ref[...] = lf

    @pl.when(i % blocks_per_seq == 0)
    def _():
        carry_ref[...] = jnp.zeros_like(carry_ref)

    half = tm // 2
    ltri = _ones_where(_iota((half, half), 1) <= _iota((half, half), 0))
    parts = jnp.concatenate(_split3(lf), axis=1)

    def fold(c):
        return c[:, 0:128] + c[:, 128:256] + c[:, 256:384]

    f0 = fold(_dot(ltri, parts[:half])) + carry_ref[0:1, :]
    f1 = fold(_dot(ltri, parts[half:])) + f0[half - 1:half, :]
    carry_ref[...] = jnp.broadcast_to(f1[half - 1:half, :], carry_ref.shape)
    f = jnp.concatenate([f0, f1], axis=0) * LOG2E

    xk = _dot(jnp.concatenate(_split3(f), axis=1), place_ref[...]) + one_ref[...]
    qx_ref[...] = xk[:, 0:128].astype(BF16)
    kx_ref[...] = xk[:, 128:256].astype(BF16)


def _inproj(x3, shift, scale, g, w, wvt, wm, bm, wa, ba, place, one, *, tm, blocks_per_seq, n_prompt):
    NG, R, D = x3.shape
    T = NG * R
    G = tm // R
    nb = T // tm
    nbp = n_prompt // tm
    row = lambda n: pl.BlockSpec((tm, n), lambda i: (i, 0))
    prow = lambda n: pl.BlockSpec((tm, n), lambda i: (jnp.minimum(i, nbp - 1), 0))
    srow = lambda n: pl.BlockSpec((tm, n), lambda i: (jnp.maximum(i - nbp, 0), 0))
    full = lambda a: pl.BlockSpec(a.shape, lambda i: (0,) * a.ndim)
    outs = [
        (jax.ShapeDtypeStruct((T, 512), BF16), row(512)),
        (jax.ShapeDtypeStruct((n_prompt, 512), F32), prow(512)),
        (jax.ShapeDtypeStruct((T - n_prompt, 512), F32), srow(512)),
        (jax.ShapeDtypeStruct((n_prompt, 512), F32), prow(512)),
        (jax.ShapeDtypeStruct((T - n_prompt, 512), F32), srow(512)),
        (jax.ShapeDtypeStruct((T, 512), BF16), row(512)),
        (jax.ShapeDtypeStruct((512, T), BF16), pl.BlockSpec((512, tm), lambda i: (0, i))),
        (jax.ShapeDtypeStruct((T, 128), BF16), row(128)),
        (jax.ShapeDtypeStruct((T, 128), BF16), row(128)),
        (jax.ShapeDtypeStruct((T, 256), F32), row(256)),
        (jax.ShapeDtypeStruct((T, 256), F32), row(256)),
        (jax.ShapeDtypeStruct((T, 512), BF16), row(512)),
        (jax.ShapeDtypeStruct((T, 512), F32), row(512)),
        (jax.ShapeDtypeStruct((T, 256), F32), row(256)),
        (jax.ShapeDtypeStruct((T, 128), F32), row(128)),
    ]
    return pl.pallas_call(
        functools.partial(_inproj_kernel, blocks_per_seq=blocks_per_seq, nbp=nbp),
        grid=(nb,),
        in_specs=[pl.BlockSpec((G, R, D), lambda i: (i, 0, 0))]
                 + _mod_specs(G, D, nbp, blocks_per_seq) + _mod_specs(G, D, nbp, blocks_per_seq)
                 + [full(g), full(w), full(wvt), full(wm), full(bm), full(wa), full(ba), full(place), full(one)],
        out_specs=[o[1] for o in outs],
        out_shape=[o[0] for o in outs],
        scratch_shapes=[pltpu.VMEM((8, 128), F32)],
        compiler_params=_cp("arbitrary"),
    )(x3, *shift, *scale, g, w, wvt, wm, bm, wa, ba, place, one)


def _fox_prompt_kernel(q_ref, qx_ref, k_ref, kx_ref, vt_ref, o_ref, sa_scr, sb_scr, m_scr, l_scr, acc_scr, *, t):
    p = pl.program_id(1)
    i = pl.program_id(2)
    lane = _iota((1, 128), 1)
    q2 = q_ref[...]
    qx = qx_ref[...]
    zero = jnp.zeros_like(q2)
    qa = []
    for hh in range(2):
        mine = (lane < FOX_DH) if hh == 0 else (lane >= FOX_DH)
        qa.append(jnp.concatenate([jnp.where(mine, q2, zero),
                                   jnp.where(lane // 16 == 2 * p + hh, qx, zero)], axis=1))
    m_scr[...] = jnp.full(m_scr.shape, NEG, F32)
    l_scr[...] = jnp.zeros_like(l_scr)
    acc_scr[...] = jnp.zeros_like(acc_scr)

    def key_start(kb):
        return pl.multiple_of(kb * t, t)

    def scores(kb):
        k0 = key_start(kb)
        ka = jnp.concatenate([k_ref[pl.ds(k0, t), :], kx_ref[pl.ds(k0, t), :]], axis=1)
        return [_dot_nt(ka, qa[hh]) for hh in range(2)]

    def scores_into(kb, buf, diagonal=False):
        st = scores(kb)
        if diagonal:
            visible = _iota((t, t), 0) <= _iota((t, t), 1)
            st = [jnp.where(visible, s_, NEG) for s_ in st]
        for hh in range(2):
            buf[hh] = st[hh]

    def consume(kb, st):
        k0 = key_start(kb)
        for hh in range(2):
            m_old = m_scr[hh]
            m_new = jnp.maximum(m_old, jnp.max(st[hh], axis=0, keepdims=True))
            alpha = jnp.exp2(m_old - m_new)
            pt = jnp.exp2(st[hh] - m_new)
            l_scr[hh] = alpha * l_scr[hh] + jnp.sum(pt, axis=0, keepdims=True)
            m_scr[hh] = m_new
            rows = slice(FOX_DH * hh, FOX_DH * (hh + 1))
            acc_scr[rows, :] = acc_scr[rows, :] * alpha + _dot(vt_ref[rows, pl.ds(k0, t)], pt.astype(BF16))

    def consume_from(kb, buf):
        consume(kb, [buf[hh] for hh in range(2)])

    @pl.when(i == 0)
    def _():
        scores_into(0, sa_scr, diagonal=True)
        consume_from(0, sa_scr)

    @pl.when(i >= 1)
    def _():
        scores_into(0, sa_scr)

    def pair(kk, c):
        kb = 2 * kk
        scores_into(kb + 1, sb_scr)
        consume_from(kb, sa_scr)
        scores_into(kb + 2, sa_scr)
        consume_from(kb + 1, sb_scr)
        return c

    lax.fori_loop(0, (i - 1) // 2, pair, 0)

    @pl.when(jnp.logical_and(i >= 1, i % 2 == 0))
    def _():
        scores_into(i - 1, sb_scr)
        consume_from(i - 2, sa_scr)
        scores_into(i, sa_scr, diagonal=True)
        consume_from(i - 1, sb_scr)
        consume_from(i, sa_scr)

    @pl.when(i % 2 == 1)
    def _():
        scores_into(i, sb_scr, diagonal=True)
        consume_from(i - 1, sa_scr)
        consume_from(i, sb_scr)

    inv =jnp.concatenate([jnp.broadcast_to(1.0 / l_scr[hh], (FOX_DH, t)) for hh in range(2)], axis=0)
    o_ref[...] = jnp.transpose(acc_scr[...] * inv).astype(o_ref.dtype)


def _fox_prompt(fq, qx, fkb, kx, vt, *, B, S, t):
    nq = S // t
    tq = t
    return pl.pallas_call(
        functools.partial(_fox_prompt_kernel, t=t),
        grid=(B, FOX_HEADS // 2, nq),
        in_specs=[pl.BlockSpec((tq, 128), lambda b, p, i: (b * nq + i, p)),
                  pl.BlockSpec((tq, 128), lambda b, p, i: (b * nq + i, 0)),
                  pl.BlockSpec((S, 128), lambda b, p, i: (b, p)),
                  pl.BlockSpec((S, 128), lambda b, p, i: (b, 0)),
                  pl.BlockSpec((128, S), lambda b, p, i: (p, b))],
        out_specs=pl.BlockSpec((tq, 128), lambda b, p, i: (b * nq + i, p)),
        out_shape=jax.ShapeDtypeStruct((B * S, 512), BF16),
        scratch_shapes=[pltpu.VMEM((2, t, t), F32), pltpu.VMEM((2, t, t), F32),
                        pltpu.VMEM((2, 1, t), F32), pltpu.VMEM((2, 1, t), F32),
                        pltpu.VMEM((128, t), F32)],
        compiler_params=_cp("parallel", "parallel", "arbitrary"),
    )(fq, qx, fkb, kx, vt)


def _suffix_sums(x, carry):
    n = x.shape[1]
    nch = n // 256
    xs = jnp.concatenate([x[:, c * 256:(c + 1) * 256] for c in range(nch)], axis=0)
    later = _ones_where(_iota((256, 256), 0) > _iota((256, 256), 1))
    r = _dot(jnp.concatenate(_split3(xs), axis=0), later)
    rows = 8 * nch
    w = r[0:rows] + r[rows:2 * rows] + r[2 * rows:3 * rows]
    tot = w[:, 0:1] + xs[:, 0:1]
    pieces = []
    run = carry
    for c in reversed(range(nch)):
        pieces.append(w[8 * c:8 * (c + 1)] + run)
        run = run + tot[8 * c:8 * (c + 1)]
    pieces.reverse()
    return jnp.concatenate(pieces, axis=1), run


def _fox_sample_kernel(q_ref, kn_ref, vn_ref, lfn_ref, kt_ref, vt_ref, lfc_ref, o_ref,
                       qbd_scr, fn_scr, m_scr, l_scr, acc_scr, g_scr, *, tk, nkb):
    kb = pl.program_id(1)
    TN = q_ref.shape[1]
    HT = FOX_HEADS * TN
    W = FOX_HEADS * FOX_DH

    def online(s, pv):
        m_old = m_scr[...]
        m_new = jnp.maximum(m_old, jnp.max(s, axis=1, keepdims=True))
        alpha = jnp.exp2(m_old - m_new)
        p = jnp.exp2(s - m_new)
        l_scr[...] = alpha * l_scr[...] + jnp.sum(p, axis=1, keepdims=True)
        m_scr[...] = m_new
        acc_scr[...] = acc_scr[...] * alpha + pv(p.astype(BF16))

    @pl.when(kb == 0)
    def _():
        q = q_ref[0]
        qt = jnp.concatenate([q] * FOX_HEADS, axis=0)
        same = (_iota((HT, W), 0) // TN) == (_iota((HT, W), 1) // FOX_DH)
        qbd = jnp.where(same, qt, jnp.zeros_like(qt))
        qbd_scr[...] = qbd
        tri = _ones_where(_iota((TN, TN), 0) <= _iota((TN, TN), 1))
        fnt = _dot3(lfn_ref[0] * LOG2E, tri)
        rep = jnp.broadcast_to(fnt[:, None, :], (FOX_HEADS, TN, TN)).reshape(HT, TN)
        tpos = _iota((HT, TN), 0) % TN
        jpos = _iota((HT, TN), 1)
        fn_col = jnp.sum(jnp.where(tpos == jpos, rep, 0.0), axis=1, keepdims=True)
        fn_scr[...] = fn_col
        m_scr[...] = jnp.full(m_scr.shape, NEG, F32)
        l_scr[...] = jnp.zeros_like(l_scr)
        acc_scr[...] = jnp.zeros_like(acc_scr)
        g_scr[...] = jnp.zeros_like(g_scr)
        s = _dot_nt(qbd, kn_ref[0]) + fn_col - rep
        online(jnp.where(jpos <= tpos, s, NEG), lambda p: _dot(p, vn_ref[0]))

    gl, total = _suffix_sums(lfc_ref[0, 0] * LOG2E, g_scr[:, 0:1])
    g_scr[...] = jnp.broadcast_to(total, g_scr.shape)
    grep = jnp.broadcast_to(gl[:, None, :], (FOX_HEADS, TN, tk)).reshape(HT, tk)
    s = _dot(qbd_scr[...], kt_ref[0, 0].astype(BF16)) + fn_scr[...] + grep
    online(s, lambda p: _dot_nt(p, vt_ref[0, 0].astype(BF16)))

    @pl.when(kb == nkb - 1)
    def _():
        accn = acc_scr[...] * (1.0 / l_scr[...])
        lane_head = _iota((TN, W), 1) // FOX_DH
        out = jnp.zeros((TN, W), F32)
        for h in range(FOX_HEADS):
            out = out + jnp.where(lane_head == h, accn[h * TN:(h + 1) * TN, :], 0.0)
        o_ref[...] = out.astype(o_ref.dtype)


def _fox_sample(q3, kn3, vn3, lfn3, kc_t, vc_t, lfc_t, layer, *, tk):
    NB, TN, W = q3.shape
    P = kc_t.shape[3]
    nkb = P // tk
    HT = FOX_HEADS * TN
    new = lambda: pl.BlockSpec((1, TN, W), lambda b, k: (b, 0, 0))
    return pl.pallas_call(
        functools.partial(_fox_sample_kernel, tk=tk, nkb=nkb),
        grid=(NB, nkb),
        in_specs=[new(), new(), new(),
                  pl.BlockSpec((1, FOX_HEADS, TN), lambda b, k: (b, 0, 0)),
                  pl.BlockSpec((1, 1, W, tk), lambda b, k: (layer, b, 0, nkb - 1 - k)),
                  pl.BlockSpec((1, 1, W, tk), lambda b, k: (layer, b, 0, nkb - 1 - k)),
                  pl.BlockSpec((1, 1, FOX_HEADS, tk), lambda b, k: (layer, b, 0, nkb - 1 - k))],
        out_specs=pl.BlockSpec((TN, W), lambda b, k: (b, 0)),
        out_shape=jax.ShapeDtypeStruct((NB * TN, W), BF16),
        scratch_shapes=[pltpu.VMEM((HT, W), BF16), pltpu.VMEM((HT, 1), F32), pltpu.VMEM((HT, 1), F32),
                        pltpu.VMEM((HT, 1), F32), pltpu.VMEM((HT, W), F32), pltpu.VMEM((FOX_HEADS, 128), F32)],
        compiler_params=_cp("parallel", "arbitrary"),
    )(q3, kn3, vn3, lfn3, kc_t, vc_t, lfc_t)


def _gla_intra(q, k, v, la, segb, tri, ones_cv):
    C = q.shape[0]
    KW, VW = GLA_HEADS * GLA_DK, GLA_HEADS * GLA_DV
    b = _dot3_l(tri, la)
    blast = b[C - 1:C, :]
    qe = (q * jnp.exp(b)).astype(BF16)
    kd = (k * jnp.exp(blast - b)).astype(BF16)
    blocks = []
    for blk in range(C // SUB):
        r0 = blk * SUB
        bi, qi, ki = b[r0:r0 + SUB], q[r0:r0 + SUB], k[r0:r0 + SUB]
        diff = bi[None, :, :] - bi[:, None, :]
        causal = _iota((SUB, SUB, 1), 1) >= _iota((SUB, SUB, 1), 0)
        tm3 = jnp.exp(jnp.where(causal, diff, -jnp.inf)) * qi[None, :, :] * ki[:, None, :]
        arep = _dot(tm3.reshape(SUB * SUB, KW).astype(BF16), segb)
        vi = v[r0:r0 + SUB].astype(F32)
        o_blk = jnp.sum(arep.reshape(SUB, SUB, VW) * vi[:, None, :], axis=0)
        if blk > 0:
            bref = b[r0 - 1:r0]
            qs = (qi * jnp.exp(bi - bref)).astype(BF16)
            ks = (k[:r0] * jnp.exp(bref - b[:r0])).astype(BF16)
            parts = []
            for h in range(GLA_HEADS):
                ks_, vs_ = slice(h * GLA_DK, (h + 1) * GLA_DK), slice(h * GLA_DV, (h + 1) * GLA_DV)
                a = _dot_nt(qs[:, ks_], ks[:, ks_])
                parts.append(_dot(a.astype(BF16), v[:r0, vs_]))
            o_blk = o_blk + jnp.concatenate(parts, axis=1)
        blocks.append(o_blk)
    o_intra = jnp.concatenate(blocks, axis=0)
    dec = jnp.exp(_dot3_tn(la, ones_cv))
    upd = _dot_tn(kd, v)
    upd = jnp.concatenate([upd[h * GLA_DK:(h + 1) * GLA_DK, h * GLA_DV:(h + 1) * GLA_DV]
                           for h in range(GLA_HEADS)], axis=0)
    return qe, o_intra, dec, upd


def _gla_kernel(q_ref, k_ref, v_ref, la_ref, s0_ref, segb_ref, o_ref, sfin_ref, *scr, C, NC, chain, nsteps):
    if chain:
        (s_scr,) = scr

        @pl.when(pl.program_id(1) == 0)
        def _():
            s_scr[...] = s0_ref[0]

    tri = _ones_where(_iota((C, C), 1) <= _iota((C, C), 0))
    ones_cv = jnp.ones((C, GLA_DV), BF16)
    segb = segb_ref[...]
    units = []
    for u in range(NC):
        rows = slice(u * C, (u + 1) * C)
        units.append(_gla_intra(q_ref[rows, :], k_ref[rows, :], v_ref[rows, :], la_ref[rows, :], segb, tri, ones_cv))

    state = s_scr[...] if chain else None
    for u, (qe, o_intra, dec, upd) in enumerate(units):
        if not chain:
            state = s0_ref[u]
        s_bf = state.astype(BF16)
        o_inter = jnp.concatenate(
            [_dot(qe[:, h * GLA_DK:(h + 1) * GLA_DK], s_bf[h * GLA_DK:(h + 1) * GLA_DK, :]) for h in range(GLA_HEADS)],
            axis=1)
        o_ref[u * C:(u + 1) * C, :] = o_intra + o_inter
        state = dec * state + upd
        if not chain:
            sfin_ref[u] = state
    if chain:
        s_scr[...] = state

        @pl.when(pl.program_id(1) == nsteps - 1)
        def _():
            sfin_ref[0] = state


def _gla(gq, gk, gv, la, s0, segb, *, C, NC, chain, nseq, nsteps, row_block_off):
    KW, VW = GLA_HEADS * GLA_DK, GLA_HEADS * GLA_DV
    rows = NC * C
    ns = 1 if chain else NC
    tok = lambda n: pl.BlockSpec((rows, n), lambda s, c: (row_block_off + s * nsteps + c, 0))
    return pl.pallas_call(
        functools.partial(_gla_kernel, C=C, NC=NC, chain=chain, nsteps=nsteps),
        grid=(nseq, nsteps),
        in_specs=[tok(KW), tok(KW), tok(VW), tok(KW),
                  pl.BlockSpec((ns, KW, GLA_DV), lambda s, c: (s, 0, 0)),
                  pl.BlockSpec(segb.shape, lambda s, c: (0, 0))],
        out_specs=[pl.BlockSpec((rows, VW), lambda s, c: (s * nsteps + c, 0)),
                   pl.BlockSpec((ns, KW, GLA_DV), lambda s, c: (s, 0, 0))],
        out_shape=[jax.ShapeDtypeStruct((nseq * nsteps * rows, VW), F32),
                   jax.ShapeDtypeStruct((nseq * ns, KW, GLA_DV), F32)],
        scratch_shapes=[pltpu.VMEM((KW, GLA_DV), F32)] if chain else [],
        compiler_params=_cp("parallel", "arbitrary"),
    )(gq, gk, gv, la, s0, segb)


def _outproj_kernel(fop_ref, fos_ref, gop_ref, gos_ref, gr_ref, x_ref, gtp_ref, gts_ref, gg_ref, wo_ref, o_ref, *, nbp):
    G, R, D = x_ref.shape
    is_prompt = pl.program_id(0) < nbp
    fo = jnp.where(is_prompt, fop_ref[...], fos_ref[...])
    go = jnp.where(is_prompt, gop_ref[...], gos_ref[...])
    parts = []
    for h in range(GLA_HEADS):
        seg = go[:, h * GLA_DV:(h + 1) * GLA_DV]
        ms = jnp.mean(seg * seg, axis=-1, keepdims=True)
        parts.append(seg * lax.rsqrt(ms + EPS) * gg_ref[...])
    gn = jnp.concatenate(parts, axis=1) * _silu(gr_ref[...])
    mix = _dot(fo, wo_ref[0:512, :]) + _dot(gn.astype(BF16), wo_ref[512:1024, :])
    o_ref[...] = x_ref[...] + _mod(gtp_ref, gts_ref, is_prompt) * mix.reshape(G, R, D)


def _outproj(fo_p, fo_s, go_p, go_s, gr, x3, gate, gg, wo, *, tm, blocks_per_seq):
    NG, R, D = x3.shape
    G = tm // R
    nbp = fo_p.shape[0] // tm
    row = lambda n: pl.BlockSpec((tm, n), lambda i: (i, 0))
    prow = lambda n: pl.BlockSpec((tm, n), lambda i: (jnp.minimum(i, nbp - 1), 0))
    srow = lambda n: pl.BlockSpec((tm, n), lambda i: (jnp.maximum(i - nbp, 0), 0))
    return pl.pallas_call(
        functools.partial(_outproj_kernel, nbp=nbp),
        grid=(NG // G,),
        in_specs=[prow(512), srow(512), prow(512), srow(512), row(512),
                  pl.BlockSpec((G, R, D), lambda i: (i, 0, 0))]
                 + _mod_specs(G, D, nbp, blocks_per_seq)
                 + [pl.BlockSpec(gg.shape, lambda i: (0, 0)),
                    pl.BlockSpec(wo.shape, lambda i: (0, 0))],
        out_specs=pl.BlockSpec((G, R, D), lambda i: (i, 0, 0)),
        out_shape=jax.ShapeDtypeStruct(x3.shape, F32),
        compiler_params=_cp("parallel"),
    )(fo_p, fo_s, go_p, go_s, gr, x3, *gate, gg, wo)


def _ffn_kernel(x_ref, shp_ref, shs_ref, scp_ref, scs_ref, gtp_ref, gts_ref, g_ref, wg_ref, wu_ref, wd_ref, o_ref,
                h_scr, acc_scr, *, nf, nbp):
    j = pl.program_id(1)
    is_prompt = pl.program_id(0) < nbp
    G, R, D = x_ref.shape

    @pl.when(j == 0)
    def _():
        h = _norm_mod(x_ref[...], g_ref[...], _mod(shp_ref, shs_ref, is_prompt), _mod(scp_ref, scs_ref, is_prompt))
        h_scr[...] = h.reshape(G * R, D).astype(BF16)
        acc_scr[...] = jnp.zeros_like(acc_scr)

    h = h_scr[...]
    a = _dot(h, wg_ref[0].astype(BF16))
    u = _dot(h, wu_ref[0].astype(BF16))
    acc_scr[...] += _dot((_silu(a) * u).astype(BF16), wd_ref[0].astype(BF16))

    @pl.when(j == nf - 1)
    def _():
        o_ref[...] = x_ref[...] + _mod(gtp_ref, gts_ref, is_prompt) * acc_scr[...].reshape(G, R, D)


def _ffn(x3, shift, scale, gate, g, wg, wu, wd, li, *, tm, tf, n_prompt, seq):
    NG, R, D = x3.shape
    G = tm // R
    F = wg.shape[-1]
    nf = F // tf
    nbp = n_prompt // tm
    grp = lambda n: pl.BlockSpec((G, n, D), lambda i, j: (i, 0, 0))
    mods = lambda: _mod_specs(G, D, nbp, seq // tm, grid_rank=2)
    return pl.pallas_call(
        functools.partial(_ffn_kernel, nf=nf, nbp=nbp),
        grid=(NG // G, nf),
        in_specs=[grp(R)] + mods() + mods() + mods()
                 + [pl.BlockSpec(g.shape, lambda i, j: (0, 0)),
                    pl.BlockSpec((1, D, tf), lambda i, j: (li, 0, j)),
                    pl.BlockSpec((1, D, tf), lambda i, j: (li, 0, j)),
                    pl.BlockSpec((1, tf, D), lambda i, j: (li, j, 0))],
        out_specs=grp(R),
        out_shape=jax.ShapeDtypeStruct(x3.shape, F32),
        scratch_shapes=[pltpu.VMEM((tm, D), BF16), pltpu.VMEM((tm, D), F32)],
        compiler_params=_cp("parallel", "arbitrary"),
    )(x3, *shift, *scale, *gate, g, wg, wu, wd)


def _router_kernel(x_ref, shp_ref, shs_ref, scp_ref, scs_ref, g_ref, wr_ref, h_ref, idx_ref, gw_ref, *, nbp):
    G, R, D = x_ref.shape
    is_prompt = pl.program_id(0) < nbp
    h = _norm_mod(x_ref[...], g_ref[...], _mod(shp_ref, shs_ref, is_prompt), _mod(scp_ref, scs_ref, is_prompt))
    h = h.reshape(G * R, D)
    h_ref[...] = h
    hh, hm, hl = _split3(h)
    wh, wm, wl = wr_ref[0], wr_ref[1], wr_ref[2]
    logits = (_dot(hh, wh) + _dot(hh, wm) + _dot(hm, wh)
              + _dot(hh, wl) + _dot(hm, wm) + _dot(hl, wh))
    lane = _iota(logits.shape, 1)
    logits = jnp.where(lane < N_EXPERTS, logits, -jnp.inf)
    v1 = jnp.max(logits, axis=1, keepdims=True)
    i1 = jnp.min(jnp.where(logits == v1, lane, 128), axis=1, keepdims=True)
    rest = jnp.where(lane == i1, -jnp.inf, logits)
    v2 = jnp.max(rest, axis=1, keepdims=True)
    i2 = jnp.min(jnp.where(rest == v2, lane, 128), axis=1, keepdims=True)
    e = jnp.exp(v2 - v1)
    g1 = 1.0 / (1.0 + e)
    idx_ref[...] = jnp.where(lane == 0, i1, jnp.where(lane == 1, i2, 0))
    gw_ref[...] = jnp.where(lane == 0, g1, jnp.where(lane == 1, e * g1, 0.0))


def _router(x3, shift, scale, g, wr3, *, tm, n_prompt, seq):
    NG, R, D = x3.shape
    T = NG * R
    G = tm // R
    nbp = n_prompt // tm
    return pl.pallas_call(
        functools.partial(_router_kernel, nbp=nbp),
        grid=(NG // G,),
        in_specs=[pl.BlockSpec((G, R, D), lambda i: (i, 0, 0))]
                 + _mod_specs(G, D, nbp, seq // tm) + _mod_specs(G, D, nbp, seq // tm)
                 + [pl.BlockSpec(g.shape, lambda i: (0, 0)),
                    pl.BlockSpec(wr3.shape, lambda i: (0, 0, 0))],
        out_specs=[pl.BlockSpec((tm, D), lambda i: (i, 0)),
                   pl.BlockSpec((tm, 128), lambda i: (i, 0)),
                   pl.BlockSpec((tm, 128), lambda i: (i, 0))],
        out_shape=[jax.ShapeDtypeStruct((T, D), F32), jax.ShapeDtypeStruct((T, 128), I32),
                   jax.ShapeDtypeStruct((T, 128), F32)],
        compiler_params=_cp("parallel"),
    )(x3, *shift, *scale, g, wr3)


def _row_copy(src_ref, src_row, dst_ref, dst_row, sem):
    return pltpu.make_async_copy(src_ref.at[pl.ds(src_row, 1), :], dst_ref.at[pl.ds(dst_row, 1), :], sem)


def _moe_ffn_kernel(te_ref, na_ref, src_ref, dst_ref, h_hbm, wg_ref, wu_ref, wd_ref, y_hbm, xbuf, h_scr, acc, gsem, ssem,
                    *, nf, tmE, chunk):
    del te_ref
    i, j = pl.program_id(0), pl.program_id(1)
    na = na_ref[0]
    slot = i % 2
    other = 1 - slot
    r0 = j * chunk

    def send_rows():
        base = i * tmE + r0
        for u in range(chunk):
            _row_copy(acc.at[other], r0 + u, y_hbm, dst_ref[base + u], ssem.at[slot]).start()

    @pl.when(jnp.logical_and(i == 0, j == 0))
    def _():
        def issue(r, c):
            _row_copy(h_hbm, src_ref[r], xbuf.at[0], r, gsem.at[0]).start()
            return c

        lax.fori_loop(0, tmE, issue, 0)
        acc[1] = jnp.zeros((tmE, acc.shape[2]), F32)

    @pl.when(jnp.logical_and(i <= na, j == 0))
    def _():
        pltpu.make_async_copy(h_hbm.at[pl.ds(0, tmE), :], xbuf.at[slot], gsem.at[slot]).wait()

        @pl.when(i >= 1)
        def _():
            pltpu.make_async_copy(acc.at[slot], y_hbm.at[pl.ds(0, tmE), :], ssem.at[other]).wait()

        h_scr[...] = xbuf[slot].astype(BF16)
        acc[slot] = jnp.zeros((tmE, acc.shape[2]), F32)

    @pl.when(i < na)
    def _():
        base = (i + 1) * tmE + r0
        for u in range(chunk):
            _row_copy(h_hbm, src_ref[base + u], xbuf.at[other], r0 + u, gsem.at[other]).start()
        send_rows()
        h = h_scr[...]
        a = _dot(h, wg_ref[0, 0].astype(BF16))
        u = _dot(h, wu_ref[0, 0].astype(BF16))
        acc[slot] += _dot((_silu(a) * u).astype(BF16), wd_ref[0, 0].astype(BF16))

    @pl.when(i == na)
    def _():
        send_rows()

        @pl.when(j == nf - 1)
        def _():
            pltpu.make_async_copy(acc.at[other], y_hbm.at[pl.ds(0, tmE), :], ssem.at[slot]).wait()


def _moe_ffn(tile_expert, n_active, src, dst, h, wg, wu, wd, li, *, n_tiles, tmE, tf, chunk):
    T, D = h.shape
    F = wg.shape[-1]
    nf = F // tf
    assert chunk * nf == tmE
    jj = lambda i, j, na: jnp.where(i < na[0], j, nf - 1)
    w_in = lambda: pl.BlockSpec((1, 1, D, tf), lambda i, j, te, na, sr, ds: (li, te[i], 0, jj(i, j, na)))
    return pl.pallas_call(
        functools.partial(_moe_ffn_kernel, nf=nf, tmE=tmE, chunk=chunk),
        grid_spec=pltpu.PrefetchScalarGridSpec(
            num_scalar_prefetch=4,
            grid=(n_tiles, nf),
            in_specs=[pl.BlockSpec(memory_space=pl.ANY), w_in(), w_in(),
                      pl.BlockSpec((1, 1, tf, D), lambda i, j, te, na, sr, ds: (li, te[i], jj(i, j, na), 0))],
            out_specs=pl.BlockSpec(memory_space=pl.ANY),
            scratch_shapes=[pltpu.VMEM((2, tmE, D), F32), pltpu.VMEM((tmE, D), BF16), pltpu.VMEM((2, tmE, D), F32),
                            pltpu.SemaphoreType.DMA((2,)), pltpu.SemaphoreType.DMA((2,))]),
        out_shape=jax.ShapeDtypeStruct((2 * T + tmE, D), F32),
        compiler_params=_cp("arbitrary", "arbitrary"),
    )(tile_expert, n_active, src, dst, h, wg, wu, wd)


def _combine_kernel(y0_ref, y1_ref, x_ref, gtp_ref, gts_ref, gw_ref, o_ref, *, nbp):
    G, R, D = x_ref.shape
    gw = gw_ref[...]
    y = gw[:, 0:1] * y0_ref[...] + gw[:, 1:2] * y1_ref[...]
    o_ref[...] = x_ref[...] + _mod(gtp_ref, gts_ref, pl.program_id(0) < nbp) * y.reshape(G, R, D)


def _combine(y2, x3, gate, gw, *, tm, n_prompt, seq):
    NG, R, D = x3.shape
    G = tm // R
    nb = NG // G
    nbp = n_prompt // tm
    return pl.pallas_call(
        functools.partial(_combine_kernel, nbp=nbp),
        grid=(nb,),
        in_specs=[pl.BlockSpec((tm, D), lambda i: (i, 0)),
                  pl.BlockSpec((tm, D), lambda i: (nb + i, 0)),
                  pl.BlockSpec((G, R, D), lambda i: (i, 0, 0))]
                 + _mod_specs(G, D, nbp, seq // tm)
                 + [pl.BlockSpec((tm, 128), lambda i: (i, 0))],
        out_specs=pl.BlockSpec((G, R, D), lambda i: (i, 0, 0)),
        out_shape=jax.ShapeDtypeStruct(x3.shape, F32),
        compiler_params=_cp("parallel"),
    )(y2, y2, x3, *gate, gw)


def _moe_plan(idx2, *, tmE, n_tiles):
    T = idx2.shape[0]
    R = n_tiles * tmE
    flat = idx2.reshape(-1)
    onehot = (flat[:, None] == jnp.arange(N_EXPERTS, dtype=I32)[None, :]).astype(I32)
    csum = jnp.cumsum(onehot, axis=0)
    rank = jnp.sum(csum * onehot, axis=1) - 1
    counts = csum[-1]
    padded = ((counts + tmE - 1) // tmE) * tmE
    ends = jnp.cumsum(padded)
    starts = ends - padded
    pos = jnp.sum(starts[None, :] * onehot, axis=1) + rank
    entry = jnp.full((R,), -1, I32).at[pos].set(jnp.arange(2 * T, dtype=I32))
    real = entry >= 0
    spare = 2 * T + jnp.arange(R, dtype=I32) % tmE
    src = jnp.where(real, entry // 2, 0)
    dst = jnp.where(real, (entry % 2) * T + entry // 2, spare)
    dst = jnp.concatenate([spare[:tmE], dst])
    n_active = ends[-1] // tmE
    tile_start = jnp.arange(n_tiles, dtype=I32) * tmE
    te = jnp.sum((tile_start[:, None] >= ends[None, :]).astype(I32), axis=1)
    last = jnp.sum(((n_active - 1) * tmE >= ends).astype(I32))
    te = jnp.where(jnp.arange(n_tiles) < n_active, te, last).astype(I32)
    return te, n_active.astype(I32).reshape(1), src.astype(I32), dst.astype(I32)


def _moe(x3, shift, scale, gate, g, wr, wg, wu, wd, li, *, tm, tmE, tf, chunk, n_prompt, seq):
    NG, R, D = x3.shape
    T = NG * R
    wr_pad = jnp.zeros((D, 128), F32).at[:, :N_EXPERTS].set(wr)
    wr3 = jnp.stack(_split3(wr_pad))
    h, idx, gw = _router(x3, shift, scale, g, wr3, tm=tm, n_prompt=n_prompt, seq=seq)
    n_tiles = -(-(2 * T) // tmE) + N_EXPERTS
    te, n_active, src, dst = _moe_plan(idx[:, :2], tmE=tmE, n_tiles=n_tiles)
    y2 = _moe_ffn(te, n_active, src, dst, h, wg, wu, wd, li, n_tiles=n_tiles, tmE=tmE, tf=tf, chunk=chunk)
    return _combine(y2, x3, gate, gw, tm=tm, n_prompt=n_prompt, seq=seq)


def _final_kernel(x_ref, g_ref, op_ref, os_ref, *, nbp):
    x = x_ref[...]
    ms = jnp.mean(x * x, axis=-1, keepdims=True)
    y = x * lax.rsqrt(ms + EPS) * g_ref[...]
    is_prompt = pl.program_id(0) < nbp

    @pl.when(is_prompt)
    def _():
        op_ref[...] = y

    @pl.when(jnp.logical_not(is_prompt))
    def _():
        os_ref[...] = y


def _final_norm(x2, g, *, tm, n_prompt):
    T, D = x2.shape
    nbp = n_prompt // tm
    return pl.pallas_call(
        functools.partial(_final_kernel, nbp=nbp),
        grid=(T // tm,),
        in_specs=[pl.BlockSpec((tm, D), lambda i: (i, 0)), pl.BlockSpec(g.shape, lambda i: (0, 0))],
        out_specs=[pl.BlockSpec((tm, D), lambda i: (jnp.minimum(i, nbp - 1), 0)),
                   pl.BlockSpec((tm, D), lambda i: (jnp.maximum(i - nbp, 0), 0))],
        out_shape=[jax.ShapeDtypeStruct((n_prompt, D), F32), jax.ShapeDtypeStruct((T - n_prompt, D), F32)],
        compiler_params=_cp("arbitrary"),
    )(x2, g)


def _segment_sum_matrix():
    r = jnp.arange(GLA_HEADS * GLA_DK)[:, None] // GLA_DK
    c = jnp.arange(GLA_HEADS * GLA_DV)[None, :] // GLA_DV
    return (r == c).astype(BF16)


def _bias_placement():
    place = jnp.zeros((384, 256), F32)
    one = jnp.zeros((1, 256), F32)
    for h in range(FOX_HEADS):
        for part in range(3):
            place = place.at[128 * part + FF_LANE + h, 16 * h + part].set(1.0)
            place = place.at[128 * part + FF_LANE + h, 128 + 16 * h + 3 + part].set(-1.0)
            one = one.at[0, 16 * h + 3 + part].set(1.0)
            one = one.at[0, 128 + 16 * h + part].set(1.0)
    return place.astype(BF16), one


def _pack_in_weights(w_in_l, b_f_l, w_a2_l, b_a_l):
    D = w_in_l.shape[0]
    o_ff, o_g = 1536, 1544
    o_ga = o_g + 256 + 256 + 512 + 512
    w = jnp.concatenate([w_in_l[:, :o_ff], w_in_l[:, o_g:o_ga]], axis=1).astype(BF16)
    wvt = w_in_l[:, 1024:1536].T.astype(BF16)
    wm = jnp.zeros((D, 128), F32).at[:, :GLA_RANK].set(w_in_l[:, o_ga:o_ga + GLA_RANK])
    wm = wm.at[:, FF_LANE:FF_LANE + FOX_HEADS].set(w_in_l[:, o_ff:o_g]).astype(BF16)
    bm = jnp.zeros((1, 128), F32).at[0, FF_LANE:FF_LANE + FOX_HEADS].set(b_f_l)
    wa = jnp.zeros((128, 256), F32).at[:GLA_RANK].set(w_a2_l).astype(BF16)
    return w, wvt, wm, bm, wa, b_a_l.reshape(1, -1)


def kernel(x_prompt, x_sample, cache_fox_k, cache_fox_v, cache_fox_lf, state_gla, c_prompt, c_sample, w_ada, b_ada, g_attn, g_ffn, w_in, b_fox_f, w_gla_a2, b_gla_a, g_gla, w_o, w_ffn_gate, w_ffn_up, w_ffn_down, w_router, w_moe_gate, w_moe_up, w_moe_down, g_final):
    B, S, D = x_prompt.shape
    NB, TN, _ = x_sample.shape
    L = w_in.shape[0]
    P = cache_fox_k.shape[2]
    TP, TS = B * S, NB * TN
    T = TP + TS
    NG = T // GROUP
    W = FOX_HEADS * FOX_DH
    KW = GLA_HEADS * GLA_DK
    tm = 512
    gla_nc = 4

    x3 = jnp.concatenate([x_prompt.reshape(TP, D), x_sample.reshape(TS, D)], axis=0).reshape(NG, GROUP, D)

    n_c = B + NB
    c_all = jnp.zeros((-(-n_c // 8) * 8, D), F32).at[:n_c].set(jnp.concatenate([c_prompt, c_sample], axis=0))
    mods = _ada(c_all, w_ada, b_ada)[:, :n_c].reshape(L, n_c, 6, D)

    segb = _segment_sum_matrix()
    place, one = _bias_placement()
    kc_all = jnp.transpose(cache_fox_k, (0, 1, 3, 4, 2)).reshape(L, NB, W, P)
    vc_all = jnp.transpose(cache_fox_v, (0, 1, 3, 4, 2)).reshape(L, NB, W, P)
    lfc_all = jnp.swapaxes(cache_fox_lf, 2, 3)
    s_prompt0 = jnp.zeros((B, KW, GLA_DV), F32)

    outs = {n: [] for n in ("kp", "vp", "lp", "sp", "kn", "vn", "ln", "sn")}
    for l in range(L):
        m = [(mods[l, :B, j][:, None, :], mods[l, B:, j][:, None, :]) for j in range(6)]
        w, wvt, wm, bm, wa, ba = _pack_in_weights(w_in[l], b_fox_f[l], w_gla_a2[l], b_gla_a[l])
        (fq, fk_p, fk_s, fv_p, fv_s, fkb, vt, qx, kx, gq, gk, gv, gr, la, lf) = _inproj(
            x3, m[0], m[1], g_attn[l].reshape(1, D), w, wvt, wm, bm, wa, ba, place, one,
            tm=tm, blocks_per_seq=S // tm, n_prompt=TP)
        lf_tok = lf[:, FF_LANE:FF_LANE + FOX_HEADS]

        fo_p = _fox_prompt(fq, qx, fkb, kx, vt, B=B, S=S, t=512)
        lfn3 = lf_tok[TP:].reshape(NB, TN, FOX_HEADS).transpose(0, 2, 1)
        fo_s = _fox_sample(fq[TP:].reshape(NB, TN, W), fkb[TP:].reshape(NB, TN, W),
                           fv_s.astype(BF16).reshape(NB, TN, W),
                           lfn3, kc_all, vc_all, lfc_all, l, tk=1024)

        go_p, s_p = _gla(gq, gk, gv, la, s_prompt0, segb, C=64, NC=gla_nc, chain=True, nseq=B,
                         nsteps=S // (64 * gla_nc), row_block_off=0)
        go_s, s_n = _gla(gq, gk, gv, la, state_gla[l].reshape(NB, KW, GLA_DV), segb, C=TN, NC=gla_nc, chain=False,
                         nseq=NB // gla_nc, nsteps=1, row_block_off=TP // (TN * gla_nc))

        x3 = _outproj(fo_p, fo_s, go_p, go_s, gr, x3, m[2], g_gla[l].reshape(1, GLA_DV), w_o[l].astype(BF16),
                      tm=tm, blocks_per_seq=S // tm)

        if l % 2 == 0:
            x3 = _ffn(x3, m[3], m[4], m[5], g_ffn[l].reshape(1, D), w_ffn_gate, w_ffn_up, w_ffn_down, l // 2,
                      tm=1024, tf=256, n_prompt=TP, seq=S)
        else:
            x3 = _moe(x3, m[3], m[4], m[5], g_ffn[l].reshape(1, D), w_router[l // 2], w_moe_gate, w_moe_up,
                      w_moe_down, l // 2, tm=tm, tmE=1056, tf=256, chunk=96, n_prompt=TP, seq=S)

        outs["kp"].append(fk_p.reshape(B, S, FOX_HEADS, FOX_DH))
        outs["vp"].append(fv_p.reshape(B, S, FOX_HEADS, FOX_DH))
        outs["lp"].append(lf_tok[:TP].reshape(B, S, FOX_HEADS))
        outs["sp"].append(s_p.reshape(B, GLA_HEADS, GLA_DK, GLA_DV))
        outs["kn"].append(fk_s.reshape(NB, TN, FOX_HEADS, FOX_DH))
        outs["vn"].append(fv_s.reshape(NB, TN, FOX_HEADS, FOX_DH))
        outs["ln"].append(lf_tok[TP:].reshape(NB, TN, FOX_HEADS))
        outs["sn"].append(s_n.reshape(NB, GLA_HEADS, GLA_DK, GLA_DV))

    y_p, y_s = _final_norm(x3.reshape(T, D), g_final.reshape(1, D), tm=tm, n_prompt=TP)
    st = lambda n: jnp.stack(outs[n])
    return (y_p.reshape(B, S, D), y_s.reshape(NB, TN, D),
            st("kp"), st("vp"), st("lp"), st("sp"), st("kn"), st("vn"), st("ln"), st("sn"))
```

```python
import functools
import math

import jax
import jax.numpy as jnp
from jax import lax
from jax.experimental import pallas as pl
from jax.experimental.pallas import tpu as pltpu

F32, BF16, I32 = jnp.float32, jnp.bfloat16, jnp.int32
EPS = 1e-6
NEG = -1e30
LOG2E = math.log2(math.e)
GROUP = 32
FOX_HEADS, FOX_DH = 8, 64
GLA_HEADS, GLA_DK, GLA_DV = 4, 64, 128
GLA_RANK = 16
GLA_TAU = 16.0
N_EXPERTS = 8
SUB = 32
FF_LANE = 16
VMEM_LIMIT = 56 * 1024 * 1024


def _cp(*sem):
    return pltpu.CompilerParams(dimension_semantics=sem, vmem_limit_bytes=VMEM_LIMIT)


def _dot(a, b):
    return jnp.dot(a, b, preferred_element_type=F32)


def _dot_nt(a, b):
    return lax.dot_general(a, b, (((1,), (1,)), ((), ())), preferred_element_type=F32)


def _dot_tn(a, b):
    return lax.dot_general(a, b, (((0,), (0,)), ((), ())), preferred_element_type=F32)


def _split3(x):
    h = x.astype(BF16)
    r = x - h.astype(F32)
    m = r.astype(BF16)
    l = (r - m.astype(F32)).astype(BF16)
    return h, m, l


def _dot3(x, u):
    h, m, l = _split3(x)
    return _dot(h, u) + _dot(m, u) + _dot(l, u)


def _dot3_l(u, x):
    h, m, l = _split3(x)
    return _dot(u, h) + _dot(u, m) + _dot(u, l)


def _dot3_tn(x, u):
    h, m, l = _split3(x)
    return _dot_tn(h, u) + _dot_tn(m, u) + _dot_tn(l, u)


def _silu(x):
    return x * (1.0 / (1.0 + jnp.exp(-x)))


def _log_sigmoid(x):
    return jnp.minimum(x, 0.0) - jnp.log1p(jnp.exp(-jnp.abs(x)))


def _norm_mod(x3, g, shift, scale):
    ms = jnp.mean(x3 * x3, axis=-1, keepdims=True)
    return (x3 * lax.rsqrt(ms + EPS) * g) * (1.0 + scale) + shift


def _mod(p_ref, s_ref, is_prompt):
    return jnp.where(is_prompt, p_ref[...], s_ref[...])


def _mod_specs(G, D, nbp, bps, grid_rank=1):
    pidx = lambda i: (jnp.minimum(i, nbp - 1) // bps, 0, 0)
    sidx = lambda i: (jnp.maximum(i - nbp, 0), 0, 0)
    if grid_rank == 1:
        return [pl.BlockSpec((1, 1, D), lambda i: pidx(i)), pl.BlockSpec((G, 1, D), lambda i: sidx(i))]
    return [pl.BlockSpec((1, 1, D), lambda i, j: pidx(i)), pl.BlockSpec((G, 1, D), lambda i, j: sidx(i))]


def _iota(shape, axis):
    return lax.broadcasted_iota(I32, shape, axis)


def _ones_where(cond):
    return jnp.where(cond, 1.0, 0.0).astype(BF16)


def _ada_kernel(c_ref, w_ref, b_ref, o_ref):
    a = _silu(c_ref[...]).astype(BF16)
    o_ref[0] = _dot(a, w_ref[0].astype(BF16)) + b_ref[0]


def _ada(c_all, w_ada, b_ada):
    L, D, N = w_ada.shape
    R = c_all.shape[0]
    tn = 1536
    return pl.pallas_call(
        _ada_kernel,
        grid=(L, N // tn),
        in_specs=[pl.BlockSpec((R, D), lambda l, j: (0, 0)),
                  pl.BlockSpec((1, D, tn), lambda l, j: (l, 0, j)),
                  pl.BlockSpec((1, 1, tn), lambda l, j: (l, 0, j))],
        out_specs=pl.BlockSpec((1, R, tn), lambda l, j: (l, 0, j)),
        out_shape=jax.ShapeDtypeStruct((L, R, N), F32),
        compiler_params=_cp("parallel", "parallel"),
    )(c_all, w_ada, b_ada.reshape(L, 1, N))


def _inproj_kernel(x_ref, shp_ref, shs_ref, scp_ref, scs_ref, g_ref, w_ref, wvt_ref, wm_ref, bm_ref, wa_ref, ba_ref,
                   place_ref, one_ref,
                   fq_ref, fkp_ref, fks_ref, fvp_ref, fvs_ref, fkb_ref, vt_ref, qx_ref, kx_ref, gq_ref, gk_ref, gv_ref,
                   gr_ref, la_ref, lf_ref, carry_ref, *, blocks_per_seq, nbp):
    i = pl.program_id(0)
    is_prompt = i < nbp
    G, R, D = x_ref.shape
    tm = G * R
    h = _norm_mod(x_ref[...], g_ref[...], _mod(shp_ref, shs_ref, is_prompt), _mod(scp_ref, scs_ref, is_prompt))
    h = h.reshape(tm, D).astype(BF16)

    fq_ref[...] = (_dot(h, w_ref[:, 0:512]) * (LOG2E * FOX_DH ** -0.5)).astype(BF16)
    zk = _dot(h, w_ref[:, 512:1024])
    fkb_ref[...] = zk.astype(BF16)
    zv = _dot(h, w_ref[:, 1024:1536])

    @pl.when(is_prompt)
    def _():
        fkp_ref[...] = zk
        fvp_ref[...] = zv

    @pl.when(jnp.logical_not(is_prompt))
    def _():
        fks_ref[...] = zk
        fvs_ref[...] = zv

    vt_ref[...] = _dot_nt(wvt_ref[...], h).astype(BF16)
    gq_ref[...] = _dot(h, w_ref[:, 1536:1792]) * (GLA_DK ** -0.5)
    gk_ref[...] = _dot(h, w_ref[:, 1792:2048])
    gv_ref[...] = _dot(h, w_ref[:, 2048:2560]).astype(BF16)
    gr_ref[...] = _dot(h, w_ref[:, 2560:3072])

    zm = _dot(h, wm_ref[...])
    la_pre = _dot(zm.astype(BF16), wa_ref[...]) + ba_ref[...]
    la_ref[...] = _log_sigmoid(la_pre) * (1.0 / GLA_TAU)
    lf = _log_sigmoid(zm + bm_ref[...])
    lf_ref[...] = lf

    @pl.when(i % blocks_per_seq == 0)
    def _():
        carry_ref[...] = jnp.zeros_like(carry_ref)

    half = tm // 2
    ltri = _ones_where(_iota((half, half), 1) <= _iota((half, half), 0))
    parts = jnp.concatenate(_split3(lf), axis=1)

    def fold(c):
        return c[:, 0:128] + c[:, 128:256] + c[:, 256:384]

    f0 = fold(_dot(ltri, parts[:half])) + carry_ref[0:1, :]
    f1 = fold(_dot(ltri, parts[half:])) + f0[half - 1:half, :]
    carry_ref[...] = jnp.broadcast_to(f1[half - 1:half, :], carry_ref.shape)
    f = jnp.concatenate([f0, f1], axis=0) * LOG2E

    xk = _dot(jnp.concatenate(_split3(f), axis=1), place_ref[...]) + one_ref[...]
    qx_ref[...] = xk[:, 0:128].astype(BF16)
    kx_ref[...] = xk[:, 128:256].astype(BF16)


def _inproj(x3, shift, scale, g, w, wvt, wm, bm, wa, ba, place, one, *, tm, blocks_per_seq, n_prompt):
    NG, R, D = x3.shape
    T = NG * R
    G = tm // R
    nb = T // tm
    nbp = n_prompt // tm
    row = lambda n: pl.BlockSpec((tm, n), lambda i: (i, 0))
    prow = lambda n: pl.BlockSpec((tm, n), lambda i: (jnp.minimum(i, nbp - 1), 0))
    srow = lambda n: pl.BlockSpec((tm, n), lambda i: (jnp.maximum(i - nbp, 0), 0))
    full = lambda a: pl.BlockSpec(a.shape, lambda i: (0,) * a.ndim)
    outs = [
        (jax.ShapeDtypeStruct((T, 512), BF16), row(512)),
        (jax.ShapeDtypeStruct((n_prompt, 512), F32), prow(512)),
        (jax.ShapeDtypeStruct((T - n_prompt, 512), F32), srow(512)),
        (jax.ShapeDtypeStruct((n_prompt, 512), F32), prow(512)),
        (jax.ShapeDtypeStruct((T - n_prompt, 512), F32), srow(512)),
        (jax.ShapeDtypeStruct((T, 512), BF16), row(512)),
        (jax.ShapeDtypeStruct((512, T), BF16), pl.BlockSpec((512, tm), lambda i: (0, i))),
        (jax.ShapeDtypeStruct((T, 128), BF16), row(128)),
        (jax.ShapeDtypeStruct((T, 128), BF16), row(128)),
        (jax.ShapeDtypeStruct((T, 256), F32), row(256)),
        (jax.ShapeDtypeStruct((T, 256), F32), row(256)),
        (jax.ShapeDtypeStruct((T, 512), BF16), row(512)),
        (jax.ShapeDtypeStruct((T, 512), F32), row(512)),
        (jax.ShapeDtypeStruct((T, 256), F32), row(256)),
        (jax.ShapeDtypeStruct((T, 128), F32), row(128)),
    ]
    return pl.pallas_call(
        functools.partial(_inproj_kernel, blocks_per_seq=blocks_per_seq, nbp=nbp),
        grid=(nb,),
        in_specs=[pl.BlockSpec((G, R, D), lambda i: (i, 0, 0))]
                 + _mod_specs(G, D, nbp, blocks_per_seq) + _mod_specs(G, D, nbp, blocks_per_seq)
                 + [full(g), full(w), full(wvt), full(wm), full(bm), full(wa), full(ba), full(place), full(one)],
        out_specs=[o[1] for o in outs],
        out_shape=[o[0] for o in outs],
        scratch_shapes=[pltpu.VMEM((8, 128), F32)],
        compiler_params=_cp("arbitrary"),
    )(x3, *shift, *scale, g, w, wvt, wm, bm, wa, ba, place, one)


def _fox_prompt_kernel(q_ref, qx_ref, k_ref, kx_ref, vt_ref, o_ref, sa_scr, sb_scr, m_scr, l_scr, acc_scr, *, t):
    p = pl.program_id(1)
    i = pl.program_id(2)
    lane = _iota((1, 128), 1)
    q2 = q_ref[...]
    qx = qx_ref[...]
    zero = jnp.zeros_like(q2)
    qa = []
    for hh in range(2):
        mine = (lane < FOX_DH) if hh == 0 else (lane >= FOX_DH)
        qa.append(jnp.concatenate([jnp.where(mine, q2, zero),
                                   jnp.where(lane // 16 == 2 * p + hh, qx, zero)], axis=1))
    m_scr[...] = jnp.full(m_scr.shape, NEG, F32)
    l_scr[...] = jnp.zeros_like(l_scr)
    acc_scr[...] = jnp.zeros_like(acc_scr)

    def key_start(kb):
        return pl.multiple_of(kb * t, t)

    def scores(kb):
        k0 = key_start(kb)
        ka = jnp.concatenate([k_ref[pl.ds(k0, t), :], kx_ref[pl.ds(k0, t), :]], axis=1)
        return [_dot_nt(ka, qa[hh]) for hh in range(2)]

    def scores_into(kb, buf, diagonal=False):
        st = scores(kb)
        if diagonal:
            visible = _iota((t, t), 0) <= _iota((t, t), 1)
            st = [jnp.where(visible, s_, NEG) for s_ in st]
        for hh in range(2):
            buf[hh] = st[hh]

    def consume(kb, st):
        k0 = key_start(kb)
        for hh in range(2):
            m_old = m_scr[hh]
            m_new = jnp.maximum(m_old, jnp.max(st[hh], axis=0, keepdims=True))
            alpha = jnp.exp2(m_old - m_new)
            pt = jnp.exp2(st[hh] - m_new)
            l_scr[hh] = alpha * l_scr[hh] + jnp.sum(pt, axis=0, keepdims=True)
            m_scr[hh] = m_new
            rows = slice(FOX_DH * hh, FOX_DH * (hh + 1))
            acc_scr[rows, :] = acc_scr[rows, :] * alpha + _dot(vt_ref[rows, pl.ds(k0, t)], pt.astype(BF16))

    def consume_from(kb, buf):
        consume(kb, [buf[hh] for hh in range(2)])

    @pl.when(i == 0)
    def _():
        scores_into(0, sa_scr, diagonal=True)
        consume_from(0, sa_scr)

    @pl.when(i >= 1)
    def _():
        scores_into(0, sa_scr)

    def pair(kk, c):
        kb = 2 * kk
        scores_into(kb + 1, sb_scr)
        consume_from(kb, sa_scr)
        scores_into(kb + 2, sa_scr)
        consume_from(kb + 1, sb_scr)
        return c

    lax.fori_loop(0, (i - 1) // 2, pair, 0)

    @pl.when(jnp.logical_and(i >= 1, i % 2 == 0))
    def _():
        scores_into(i - 1, sb_scr)
        consume_from(i - 2, sa_scr)
        scores_into(i, sa_scr, diagonal=True)
        consume_from(i - 1, sb_scr)
        consume_from(i, sa_scr)

    @pl.when(i % 2 == 1)
    def _():
        scores_into(i, sb_scr, diagonal=True)
        consume_from(i - 1, sa_scr)
        consume_from(i, sb_scr)

    inv =jnp.concatenate([jnp.broadcast_to(1.0 / l_scr[hh], (FOX_DH, t)) for hh in range(2)], axis=0)
    o_ref[...] = jnp.transpose(acc_scr[...] * inv).astype(o_ref.dtype)


def _fox_prompt(fq, qx, fkb, kx, vt, *, B, S, t):
    nq = S // t
    tq = t
    return pl.pallas_call(
        functools.partial(_fox_prompt_kernel, t=t),
        grid=(B, FOX_HEADS // 2, nq),
        in_specs=[pl.BlockSpec((tq, 128), lambda b, p, i: (b * nq + i, p)),
                  pl.BlockSpec((tq, 128), lambda b, p, i: (b * nq + i, 0)),
                  pl.BlockSpec((S, 128), lambda b, p, i: (b, p)),
                  pl.BlockSpec((S, 128), lambda b, p, i: (b, 0)),
                  pl.BlockSpec((128, S), lambda b, p, i: (p, b))],
        out_specs=pl.BlockSpec((tq, 128), lambda b, p, i: (b * nq + i, p)),
        out_shape=jax.ShapeDtypeStruct((B * S, 512), BF16),
        scratch_shapes=[pltpu.VMEM((2, t, t), F32), pltpu.VMEM((2, t, t), F32),
                        pltpu.VMEM((2, 1, t), F32), pltpu.VMEM((2, 1, t), F32),
                        pltpu.VMEM((128, t), F32)],
        compiler_params=_cp("parallel", "parallel", "arbitrary"),
    )(fq, qx, fkb, kx, vt)


def _suffix_sums(x, carry):
    n = x.shape[1]
    nch = n // 256
    xs = jnp.concatenate([x[:, c * 256:(c + 1) * 256] for c in range(nch)], axis=0)
    later = _ones_where(_iota((256, 256), 0) > _iota((256, 256), 1))
    r = _dot(jnp.concatenate(_split3(xs), axis=0), later)
    rows = 8 * nch
    w = r[0:rows] + r[rows:2 * rows] + r[2 * rows:3 * rows]
    tot = w[:, 0:1] + xs[:, 0:1]
    pieces = []
    run = carry
    for c in reversed(range(nch)):
        pieces.append(w[8 * c:8 * (c + 1)] + run)
        run = run + tot[8 * c:8 * (c + 1)]
    pieces.reverse()
    return jnp.concatenate(pieces, axis=1), run


def _fox_sample_kernel(q_ref, kn_ref, vn_ref, lfn_ref, kt_ref, vt_ref, lfc_ref, o_ref,
                       qbd_scr, fn_scr, m_scr, l_scr, acc_scr, g_scr, *, tk, nkb):
    kb = pl.program_id(1)
    TN = q_ref.shape[1]
    HT = FOX_HEADS * TN
    W = FOX_HEADS * FOX_DH

    def online(s, pv):
        m_old = m_scr[...]
        m_new = jnp.maximum(m_old, jnp.max(s, axis=1, keepdims=True))
        alpha = jnp.exp2(m_old - m_new)
        p = jnp.exp2(s - m_new)
        l_scr[...] = alpha * l_scr[...] + jnp.sum(p, axis=1, keepdims=True)
        m_scr[...] = m_new
        acc_scr[...] = acc_scr[...] * alpha + pv(p.astype(BF16))

    @pl.when(kb == 0)
    def _():
        q = q_ref[0]
        qt = jnp.concatenate([q] * FOX_HEADS, axis=0)
        same = (_iota((HT, W), 0) // TN) == (_iota((HT, W), 1) // FOX_DH)
        qbd = jnp.where(same, qt, jnp.zeros_like(qt))
        qbd_scr[...] = qbd
        tri = _ones_where(_iota((TN, TN), 0) <= _iota((TN, TN), 1))
        fnt = _dot3(lfn_ref[0] * LOG2E, tri)
        rep = jnp.broadcast_to(fnt[:, None, :], (FOX_HEADS, TN, TN)).reshape(HT, TN)
        tpos = _iota((HT, TN), 0) % TN
        jpos = _iota((HT, TN), 1)
        fn_col = jnp.sum(jnp.where(tpos == jpos, rep, 0.0), axis=1, keepdims=True)
        fn_scr[...] = fn_col
        m_scr[...] = jnp.full(m_scr.shape, NEG, F32)
        l_scr[...] = jnp.zeros_like(l_scr)
        acc_scr[...] = jnp.zeros_like(acc_scr)
        g_scr[...] = jnp.zeros_like(g_scr)
        s = _dot_nt(qbd, kn_ref[0]) + fn_col - rep
        online(jnp.where(jpos <= tpos, s, NEG), lambda p: _dot(p, vn_ref[0]))

    gl, total = _suffix_sums(lfc_ref[0, 0] * LOG2E, g_scr[:, 0:1])
    g_scr[...] = jnp.broadcast_to(total, g_scr.shape)
    grep = jnp.broadcast_to(gl[:, None, :], (FOX_HEADS, TN, tk)).reshape(HT, tk)
    s = _dot(qbd_scr[...], kt_ref[0, 0].astype(BF16)) + fn_scr[...] + grep
    online(s, lambda p: _dot_nt(p, vt_ref[0, 0].astype(BF16)))

    @pl.when(kb == nkb - 1)
    def _():
        accn = acc_scr[...] * (1.0 / l_scr[...])
        lane_head = _iota((TN, W), 1) // FOX_DH
        out = jnp.zeros((TN, W), F32)
        for h in range(FOX_HEADS):
            out = out + jnp.where(lane_head == h, accn[h * TN:(h + 1) * TN, :], 0.0)
        o_ref[...] = out.astype(o_ref.dtype)


def _fox_sample(q3, kn3, vn3, lfn3, kc_t, vc_t, lfc_t, layer, *, tk):
    NB, TN, W = q3.shape
    P = kc_t.shape[3]
    nkb = P // tk
    HT = FOX_HEADS * TN
    new = lambda: pl.BlockSpec((1, TN, W), lambda b, k: (b, 0, 0))
    return pl.pallas_call(
        functools.partial(_fox_sample_kernel, tk=tk, nkb=nkb),
        grid=(NB, nkb),
        in_specs=[new(), new(), new(),
                  pl.BlockSpec((1, FOX_HEADS, TN), lambda b, k: (b, 0, 0)),
                  pl.BlockSpec((1, 1, W, tk), lambda b, k: (layer, b, 0, nkb - 1 - k)),
                  pl.BlockSpec((1, 1, W, tk), lambda b, k: (layer, b, 0, nkb - 1 - k)),
                  pl.BlockSpec((1, 1, FOX_HEADS, tk), lambda b, k: (layer, b, 0, nkb - 1 - k))],
        out_specs=pl.BlockSpec((TN, W), lambda b, k: (b, 0)),
        out_shape=jax.ShapeDtypeStruct((NB * TN, W), BF16),
        scratch_shapes=[pltpu.VMEM((HT, W), BF16), pltpu.VMEM((HT, 1), F32), pltpu.VMEM((HT, 1), F32),
                        pltpu.VMEM((HT, 1), F32), pltpu.VMEM((HT, W), F32), pltpu.VMEM((FOX_HEADS, 128), F32)],
        compiler_params=_cp("parallel", "arbitrary"),
    )(q3, kn3, vn3, lfn3, kc_t, vc_t, lfc_t)


def _gla_intra(q, k, v, la, segb, tri, ones_cv):
    C = q.shape[0]
    sub = min(SUB, C)
    KW, VW = GLA_HEADS * GLA_DK, GLA_HEADS * GLA_DV
    b = _dot3_l(tri, la)
    blast = b[C - 1:C, :]
    qe = (q * jnp.exp(b)).astype(BF16)
    kd = (k * jnp.exp(blast - b)).astype(BF16)
    blocks = []
    for blk in range(C // sub):
        r0 = blk * sub
        bi, qi = b[r0:r0 + sub], q[r0:r0 + sub]
        o_blk = None
        for i0 in range(0, sub, 8):
            n_i = sub - i0
            bj, kj = b[r0 + i0:r0 + i0 + 8], k[r0 + i0:r0 + i0 + 8]
            vj = v[r0 + i0:r0 + i0 + 8].astype(F32)
            diff = bi[i0:][None, :, :] - bj[:, None, :]
            causal = _iota((8, n_i, 1), 1) >= _iota((8, n_i, 1), 0)
            tm3 = jnp.exp(jnp.where(causal, diff, -jnp.inf)) * qi[i0:][None, :, :] * kj[:, None, :]
            arep = _dot(tm3.reshape(8 * n_i, KW).astype(BF16), segb)
            og = jnp.sum(arep.reshape(8, n_i, VW) * vj[:, None, :], axis=0)
            if i0:
                og = jnp.concatenate([jnp.zeros((i0, VW), F32), og], axis=0)
            o_blk = og if o_blk is None else o_blk + og
        if blk > 0:
            bref = b[r0 - 1:r0]
            qs = (qi * jnp.exp(bi - bref)).astype(BF16)
            ks = (k[:r0] * jnp.exp(bref - b[:r0])).astype(BF16)
            parts = []
            for h in range(GLA_HEADS):
                ks_, vs_ = slice(h * GLA_DK, (h + 1) * GLA_DK), slice(h * GLA_DV, (h + 1) * GLA_DV)
                a = _dot_nt(qs[:, ks_], ks[:, ks_])
                parts.append(_dot(a.astype(BF16), v[:r0, vs_]))
            o_blk = o_blk + jnp.concatenate(parts, axis=1)
        blocks.append(o_blk)
    o_intra = jnp.concatenate(blocks, axis=0)
    dec = jnp.exp(_dot3_tn(la, ones_cv))
    upd = _dot_tn(kd, v)
    upd = jnp.concatenate([upd[h * GLA_DK:(h + 1) * GLA_DK, h * GLA_DV:(h + 1) * GLA_DV]
                           for h in range(GLA_HEADS)], axis=0)
    return qe, o_intra, dec, upd


def _gla_kernel(q_ref, k_ref, v_ref, la_ref, s0_ref, segb_ref, o_ref, sfin_ref, *scr, C, NC, chain, nsteps):
    if chain:
        (s_scr,) = scr

        @pl.when(pl.program_id(1) == 0)
        def _():
            s_scr[...] = s0_ref[0]

    tri = _ones_where(_iota((C, C), 1) <= _iota((C, C), 0))
    ones_cv = jnp.ones((C, GLA_DV), BF16)
    segb = segb_ref[...]
    units = []
    for u in range(NC):
        rows = slice(u * C, (u + 1) * C)
        units.append(_gla_intra(q_ref[rows, :], k_ref[rows, :], v_ref[rows, :], la_ref[rows, :], segb, tri, ones_cv))

    state = s_scr[...] if chain else None
    for u, (qe, o_intra, dec, upd) in enumerate(units):
        if not chain:
            state = s0_ref[u]
        s_bf = state.astype(BF16)
        o_inter = jnp.concatenate(
            [_dot(qe[:, h * GLA_DK:(h + 1) * GLA_DK], s_bf[h * GLA_DK:(h + 1) * GLA_DK, :]) for h in range(GLA_HEADS)],
            axis=1)
        o_ref[u * C:(u + 1) * C, :] = o_intra + o_inter
        state = dec * state + upd
        if not chain:
            sfin_ref[u] = state
    if chain:
        s_scr[...] = state

        @pl.when(pl.program_id(1) == nsteps - 1)
        def _():
            sfin_ref[0] = state


def _gla(gq, gk, gv, la, s0, segb, *, C, NC, chain, nseq, nsteps, row_block_off):
    KW, VW = GLA_HEADS * GLA_DK, GLA_HEADS * GLA_DV
    rows = NC * C
    ns = 1 if chain else NC
    tok = lambda n: pl.BlockSpec((rows, n), lambda s, c: (row_block_off + s * nsteps + c, 0))
    return pl.pallas_call(
        functools.partial(_gla_kernel, C=C, NC=NC, chain=chain, nsteps=nsteps),
        grid=(nseq, nsteps),
        in_specs=[tok(KW), tok(KW), tok(VW), tok(KW),
                  pl.BlockSpec((ns, KW, GLA_DV), lambda s, c: (s, 0, 0)),
                  pl.BlockSpec(segb.shape, lambda s, c: (0, 0))],
        out_specs=[pl.BlockSpec((rows, VW), lambda s, c: (s * nsteps + c, 0)),
                   pl.BlockSpec((ns, KW, GLA_DV), lambda s, c: (s, 0, 0))],
        out_shape=[jax.ShapeDtypeStruct((nseq * nsteps * rows, VW), F32),
                   jax.ShapeDtypeStruct((nseq * ns, KW, GLA_DV), F32)],
        scratch_shapes=[pltpu.VMEM((KW, GLA_DV), F32)] if chain else [],
        compiler_params=_cp("parallel", "arbitrary"),
    )(gq, gk, gv, la, s0, segb)


def _outproj_kernel(fop_ref, fos_ref, gop_ref, gos_ref, gr_ref, x_ref, gtp_ref, gts_ref, gg_ref, wo_ref, o_ref, *, nbp):
    G, R, D = x_ref.shape
    is_prompt = pl.program_id(0) < nbp
    fo = jnp.where(is_prompt, fop_ref[...], fos_ref[...])
    go = jnp.where(is_prompt, gop_ref[...], gos_ref[...])
    parts = []
    for h in range(GLA_HEADS):
        seg = go[:, h * GLA_DV:(h + 1) * GLA_DV]
        ms = jnp.mean(seg * seg, axis=-1, keepdims=True)
        parts.append(seg * lax.rsqrt(ms + EPS) * gg_ref[...])
    gn = jnp.concatenate(parts, axis=1) * _silu(gr_ref[...])
    mix = _dot(fo, wo_ref[0:512, :]) + _dot(gn.astype(BF16), wo_ref[512:1024, :])
    o_ref[...] = x_ref[...] + _mod(gtp_ref, gts_ref, is_prompt) * mix.reshape(G, R, D)


def _outproj(fo_p, fo_s, go_p, go_s, gr, x3, gate, gg, wo, *, tm, blocks_per_seq):
    NG, R, D = x3.shape
    G = tm // R
    nbp = fo_p.shape[0] // tm
    row = lambda n: pl.BlockSpec((tm, n), lambda i: (i, 0))
    prow = lambda n: pl.BlockSpec((tm, n), lambda i: (jnp.minimum(i, nbp - 1), 0))
    srow = lambda n: pl.BlockSpec((tm, n), lambda i: (jnp.maximum(i - nbp, 0), 0))
    return pl.pallas_call(
        functools.partial(_outproj_kernel, nbp=nbp),
        grid=(NG // G,),
        in_specs=[prow(512), srow(512), prow(512), srow(512), row(512),
                  pl.BlockSpec((G, R, D), lambda i: (i, 0, 0))]
                 + _mod_specs(G, D, nbp, blocks_per_seq)
                 + [pl.BlockSpec(gg.shape, lambda i: (0, 0)),
                    pl.BlockSpec(wo.shape, lambda i: (0, 0))],
        out_specs=pl.BlockSpec((G, R, D), lambda i: (i, 0, 0)),
        out_shape=jax.ShapeDtypeStruct(x3.shape, F32),
        compiler_params=_cp("parallel"),
    )(fo_p, fo_s, go_p, go_s, gr, x3, *gate, gg, wo)


def _ffn_kernel(x_ref, shp_ref, shs_ref, scp_ref, scs_ref, gtp_ref, gts_ref, g_ref, wg_ref, wu_ref, wd_ref, o_ref,
                h_scr, acc_scr, *, nf, nbp):
    j = pl.program_id(1)
    is_prompt = pl.program_id(0) < nbp
    G, R, D = x_ref.shape

    @pl.when(j == 0)
    def _():
        h = _norm_mod(x_ref[...], g_ref[...], _mod(shp_ref, shs_ref, is_prompt), _mod(scp_ref, scs_ref, is_prompt))
        h_scr[...] = h.reshape(G * R, D).astype(BF16)
        acc_scr[...] = jnp.zeros_like(acc_scr)

    h = h_scr[...]
    a = _dot(h, wg_ref[0].astype(BF16))
    u = _dot(h, wu_ref[0].astype(BF16))
    acc_scr[...] += _dot((_silu(a) * u).astype(BF16), wd_ref[0].astype(BF16))

    @pl.when(j == nf - 1)
    def _():
        o_ref[...] = x_ref[...] + _mod(gtp_ref, gts_ref, is_prompt) * acc_scr[...].reshape(G, R, D)


def _ffn(x3, shift, scale, gate, g, wg, wu, wd, li, *, tm, tf, n_prompt, seq):
    NG, R, D = x3.shape
    G = tm // R
    F = wg.shape[-1]
    nf = F // tf
    nbp = n_prompt // tm
    grp = lambda n: pl.BlockSpec((G, n, D), lambda i, j: (i, 0, 0))
    mods = lambda: _mod_specs(G, D, nbp, seq // tm, grid_rank=2)
    return pl.pallas_call(
        functools.partial(_ffn_kernel, nf=nf, nbp=nbp),
        grid=(NG // G, nf),
        in_specs=[grp(R)] + mods() + mods() + mods()
                 + [pl.BlockSpec(g.shape, lambda i, j: (0, 0)),
                    pl.BlockSpec((1, D, tf), lambda i, j: (li, 0, j)),
                    pl.BlockSpec((1, D, tf), lambda i, j: (li, 0, j)),
                    pl.BlockSpec((1, tf, D), lambda i, j: (li, j, 0))],
        out_specs=grp(R),
        out_shape=jax.ShapeDtypeStruct(x3.shape, F32),
        scratch_shapes=[pltpu.VMEM((tm, D), BF16), pltpu.VMEM((tm, D), F32)],
        compiler_params=_cp("parallel", "arbitrary"),
    )(x3, *shift, *scale, *gate, g, wg, wu, wd)


def _router_kernel(x_ref, shp_ref, shs_ref, scp_ref, scs_ref, g_ref, wr_ref, h_ref, idx_ref, gw_ref, *, nbp):
    G, R, D = x_ref.shape
    is_prompt = pl.program_id(0) < nbp
    h = _norm_mod(x_ref[...], g_ref[...], _mod(shp_ref, shs_ref, is_prompt), _mod(scp_ref, scs_ref, is_prompt))
    h = h.reshape(G * R, D)
    h_ref[...] = h
    hh, hm, hl = _split3(h)
    wh, wm, wl = wr_ref[0], wr_ref[1], wr_ref[2]
    logits = (_dot(hh, wh) + _dot(hh, wm) + _dot(hm, wh)
              + _dot(hh, wl) + _dot(hm, wm) + _dot(hl, wh))
    lane = _iota(logits.shape, 1)
    logits = jnp.where(lane < N_EXPERTS, logits, -jnp.inf)
    v1 = jnp.max(logits, axis=1, keepdims=True)
    i1 = jnp.min(jnp.where(logits == v1, lane, 128), axis=1, keepdims=True)
    rest = jnp.where(lane == i1, -jnp.inf, logits)
    v2 = jnp.max(rest, axis=1, keepdims=True)
    i2 = jnp.min(jnp.where(rest == v2, lane, 128), axis=1, keepdims=True)
    e = jnp.exp(v2 - v1)
    g1 = 1.0 / (1.0 + e)
    idx_ref[...] = jnp.where(lane == 0, i1, jnp.where(lane == 1, i2, 0))
    gw_ref[...] = jnp.where(lane == 0, g1, jnp.where(lane == 1, e * g1, 0.0))


def _router(x3, shift, scale, g, wr3, *, tm, n_prompt, seq):
    NG, R, D = x3.shape
    T = NG * R
    G = tm // R
    nbp = n_prompt // tm
    return pl.pallas_call(
        functools.partial(_router_kernel, nbp=nbp),
        grid=(NG // G,),
        in_specs=[pl.BlockSpec((G, R, D), lambda i: (i, 0, 0))]
                 + _mod_specs(G, D, nbp, seq // tm) + _mod_specs(G, D, nbp, seq // tm)
                 + [pl.BlockSpec(g.shape, lambda i: (0, 0)),
                    pl.BlockSpec(wr3.shape, lambda i: (0, 0, 0))],
        out_specs=[pl.BlockSpec((tm, D), lambda i: (i, 0)),
                   pl.BlockSpec((tm, 128), lambda i: (i, 0)),
                   pl.BlockSpec((tm, 128), lambda i: (i, 0))],
        out_shape=[jax.ShapeDtypeStruct((T, D), F32), jax.ShapeDtypeStruct((T, 128), I32),
                   jax.ShapeDtypeStruct((T, 128), F32)],
        compiler_params=_cp("parallel"),
    )(x3, *shift, *scale, g, wr3)


def _row_copy(src_ref, src_row, dst_ref, dst_row, sem):
    return pltpu.make_async_copy(src_ref.at[pl.ds(src_row, 1), :], dst_ref.at[pl.ds(dst_row, 1), :], sem)


def _moe_ffn_kernel(te_ref, na_ref, src_ref, dst_ref, h_hbm, wg_ref, wu_ref, wd_ref, y_hbm, xbuf, h_scr, acc, gsem, ssem,
                    *, nf, tmE, chunk):
    del te_ref
    i, j = pl.program_id(0), pl.program_id(1)
    na = na_ref[0]
    slot = i % 2
    other = 1 - slot
    r0 = j * chunk

    def send_rows():
        base = i * tmE + r0
        for u in range(chunk):
            _row_copy(acc.at[other], r0 + u, y_hbm, dst_ref[base + u], ssem.at[slot]).start(priority=u % 2)

    @pl.when(jnp.logical_and(i == 0, j == 0))
    def _():
        def issue(r, c):
            _row_copy(h_hbm, src_ref[r], xbuf.at[0], r, gsem.at[0]).start()
            return c

        lax.fori_loop(0, tmE, issue, 0)
        acc[1] = jnp.zeros((tmE, acc.shape[2]), F32)

    @pl.when(jnp.logical_and(i <= na, j == 0))
    def _():
        pltpu.make_async_copy(h_hbm.at[pl.ds(0, tmE), :], xbuf.at[slot], gsem.at[slot]).wait()

        @pl.when(i >= 1)
        def _():
            pltpu.make_async_copy(acc.at[slot], y_hbm.at[pl.ds(0, tmE), :], ssem.at[other]).wait()

        h_scr[...] = xbuf[slot].astype(BF16)
        acc[slot] = jnp.zeros((tmE, acc.shape[2]), F32)

    @pl.when(i < na)
    def _():
        base = (i + 1) * tmE + r0
        for u in range(chunk):
            _row_copy(h_hbm, src_ref[base + u], xbuf.at[other], r0 + u, gsem.at[other]).start()
        send_rows()
        h = h_scr[...]
        a = _dot(h, wg_ref[0, 0].astype(BF16))
        u = _dot(h, wu_ref[0, 0].astype(BF16))
        acc[slot] += _dot((_silu(a) * u).astype(BF16), wd_ref[0, 0].astype(BF16))

    @pl.when(i == na)
    def _():
        send_rows()

        @pl.when(j == nf - 1)
        def _():
            pltpu.make_async_copy(acc.at[other], y_hbm.at[pl.ds(0, tmE), :], ssem.at[slot]).wait()


def _moe_ffn(tile_expert, n_active, src, dst, h, wg, wu, wd, li, *, n_tiles, tmE, tf, chunk):
    T, D = h.shape
    F = wg.shape[-1]
    nf = F // tf
    assert chunk * nf == tmE
    jj = lambda i, j, na: jnp.where(i < na[0], j, nf - 1)
    w_in = lambda: pl.BlockSpec((1, 1, D, tf), lambda i, j, te, na, sr, ds: (li, te[i], 0, jj(i, j, na)))
    return pl.pallas_call(
        functools.partial(_moe_ffn_kernel, nf=nf, tmE=tmE, chunk=chunk),
        grid_spec=pltpu.PrefetchScalarGridSpec(
            num_scalar_prefetch=4,
            grid=(n_tiles, nf),
            in_specs=[pl.BlockSpec(memory_space=pl.ANY), w_in(), w_in(),
                      pl.BlockSpec((1, 1, tf, D), lambda i, j, te, na, sr, ds: (li, te[i], jj(i, j, na), 0))],
            out_specs=pl.BlockSpec(memory_space=pl.ANY),
            scratch_shapes=[pltpu.VMEM((2, tmE, D), F32), pltpu.VMEM((tmE, D), BF16), pltpu.VMEM((2, tmE, D), F32),
                            pltpu.SemaphoreType.DMA((2,)), pltpu.SemaphoreType.DMA((2,))]),
        out_shape=jax.ShapeDtypeStruct((2 * T + tmE, D), F32),
        compiler_params=_cp("arbitrary", "arbitrary"),
    )(tile_expert, n_active, src, dst, h, wg, wu, wd)


def _combine_kernel(y0_ref, y1_ref, x_ref, gtp_ref, gts_ref, gw_ref, o_ref, *, nbp):
    G, R, D = x_ref.shape
    gw = gw_ref[...]
    y = gw[:, 0:1] * y0_ref[...] + gw[:, 1:2] * y1_ref[...]
    o_ref[...] = x_ref[...] + _mod(gtp_ref, gts_ref, pl.program_id(0) < nbp) * y.reshape(G, R, D)


def _combine(y2, x3, gate, gw, *, tm, n_prompt, seq):
    NG, R, D = x3.shape
    G = tm // R
    nb = NG // G
    nbp = n_prompt // tm
    return pl.pallas_call(
        functools.partial(_combine_kernel, nbp=nbp),
        grid=(nb,),
        in_specs=[pl.BlockSpec((tm, D), lambda i: (i, 0)),
                  pl.BlockSpec((tm, D), lambda i: (nb + i, 0)),
                  pl.BlockSpec((G, R, D), lambda i: (i, 0, 0))]
                 + _mod_specs(G, D, nbp, seq // tm)
                 + [pl.BlockSpec((tm, 128), lambda i: (i, 0))],
        out_specs=pl.BlockSpec((G, R, D), lambda i: (i, 0, 0)),
        out_shape=jax.ShapeDtypeStruct(x3.shape, F32),
        compiler_params=_cp("parallel"),
    )(y2, y2, x3, *gate, gw)


def _moe_plan(idx2, *, tmE, n_tiles):
    T = idx2.shape[0]
    R = n_tiles * tmE
    flat = idx2.reshape(-1)
    onehot = (flat[:, None] == jnp.arange(N_EXPERTS, dtype=I32)[None, :]).astype(I32)
    csum = jnp.cumsum(onehot, axis=0)
    rank = jnp.sum(csum * onehot, axis=1) - 1
    counts = csum[-1]
    padded = ((counts + tmE - 1) // tmE) * tmE
    ends = jnp.cumsum(padded)
    starts = ends - padded
    pos = jnp.sum(starts[None, :] * onehot, axis=1) + rank
    entry = jnp.full((R,), -1, I32).at[pos].set(jnp.arange(2 * T, dtype=I32))
    real = entry >= 0
    spare = 2 * T + jnp.arange(R, dtype=I32) % tmE
    src = jnp.where(real, entry // 2, 0)
    dst = jnp.where(real, (entry % 2) * T + entry // 2, spare)
    dst = jnp.concatenate([spare[:tmE], dst])
    n_active = ends[-1] // tmE
    tile_start = jnp.arange(n_tiles, dtype=I32) * tmE
    te = jnp.sum((tile_start[:, None] >= ends[None, :]).astype(I32), axis=1)
    last = jnp.sum(((n_active - 1) * tmE >= ends).astype(I32))
    te = jnp.where(jnp.arange(n_tiles) < n_active, te, last).astype(I32)
    return te, n_active.astype(I32).reshape(1), src.astype(I32), dst.astype(I32)


def _moe(x3, shift, scale, gate, g, wr, wg, wu, wd, li, *, tm, tmE, tf, chunk, n_prompt, seq):
    NG, R, D = x3.shape
    T = NG * R
    wr_pad = jnp.zeros((D, 128), F32).at[:, :N_EXPERTS].set(wr)
    wr3 = jnp.stack(_split3(wr_pad))
    h, idx, gw = _router(x3, shift, scale, g, wr3, tm=tm, n_prompt=n_prompt, seq=seq)
    n_tiles = -(-(2 * T) // tmE) + N_EXPERTS
    te, n_active, src, dst = _moe_plan(idx[:, :2], tmE=tmE, n_tiles=n_tiles)
    y2 = _moe_ffn(te, n_active, src, dst, h, wg, wu, wd, li, n_tiles=n_tiles, tmE=tmE, tf=tf, chunk=chunk)
    return _combine(y2, x3, gate, gw, tm=tm, n_prompt=n_prompt, seq=seq)


def _final_kernel(x_ref, g_ref, op_ref, os_ref, *, nbp):
    x = x_ref[...]
    ms = jnp.mean(x * x, axis=-1, keepdims=True)
    y = x * lax.rsqrt(ms + EPS) * g_ref[...]
    is_prompt = pl.program_id(0) < nbp

    @pl.when(is_prompt)
    def _():
        op_ref[...] = y

    @pl.when(jnp.logical_not(is_prompt))
    def _():
        os_ref[...] = y


def _final_norm(x2, g, *, tm, n_prompt):
    T, D = x2.shape
    nbp = n_prompt // tm
    return pl.pallas_call(
        functools.partial(_final_kernel, nbp=nbp),
        grid=(T // tm,),
        in_specs=[pl.BlockSpec((tm, D), lambda i: (i, 0)), pl.BlockSpec(g.shape, lambda i: (0, 0))],
        out_specs=[pl.BlockSpec((tm, D), lambda i: (jnp.minimum(i, nbp - 1), 0)),
                   pl.BlockSpec((tm, D), lambda i: (jnp.maximum(i - nbp, 0), 0))],
        out_shape=[jax.ShapeDtypeStruct((n_prompt, D), F32), jax.ShapeDtypeStruct((T - n_prompt, D), F32)],
        compiler_params=_cp("arbitrary"),
    )(x2, g)


def _segment_sum_matrix():
    r = jnp.arange(GLA_HEADS * GLA_DK)[:, None] // GLA_DK
    c = jnp.arange(GLA_HEADS * GLA_DV)[None, :] // GLA_DV
    return (r == c).astype(BF16)


def _bias_placement():
    place = jnp.zeros((384, 256), F32)
    one = jnp.zeros((1, 256), F32)
    for h in range(FOX_HEADS):
        for part in range(3):
            place = place.at[128 * part + FF_LANE + h, 16 * h + part].set(1.0)
            place = place.at[128 * part + FF_LANE + h, 128 + 16 * h + 3 + part].set(-1.0)
            one = one.at[0, 16 * h + 3 + part].set(1.0)
            one = one.at[0, 128 + 16 * h + part].set(1.0)
    return place.astype(BF16), one


def _pack_in_weights(w_in_l, b_f_l, w_a2_l, b_a_l):
    D = w_in_l.shape[0]
    o_ff, o_g = 1536, 1544
    o_ga = o_g + 256 + 256 + 512 + 512
    w = jnp.concatenate([w_in_l[:, :o_ff], w_in_l[:, o_g:o_ga]], axis=1).astype(BF16)
    wvt = w_in_l[:, 1024:1536].T.astype(BF16)
    wm = jnp.zeros((D, 128), F32).at[:, :GLA_RANK].set(w_in_l[:, o_ga:o_ga + GLA_RANK])
    wm = wm.at[:, FF_LANE:FF_LANE + FOX_HEADS].set(w_in_l[:, o_ff:o_g]).astype(BF16)
    bm = jnp.zeros((1, 128), F32).at[0, FF_LANE:FF_LANE + FOX_HEADS].set(b_f_l)
    wa = jnp.zeros((128, 256), F32).at[:GLA_RANK].set(w_a2_l).astype(BF16)
    return w, wvt, wm, bm, wa, b_a_l.reshape(1, -1)


def kernel(x_prompt, x_sample, cache_fox_k, cache_fox_v, cache_fox_lf, state_gla, c_prompt, c_sample, w_ada, b_ada, g_attn, g_ffn, w_in, b_fox_f, w_gla_a2, b_gla_a, g_gla, w_o, w_ffn_gate, w_ffn_up, w_ffn_down, w_router, w_moe_gate, w_moe_up, w_moe_down, g_final):
    B, S, D = x_prompt.shape
    NB, TN, _ = x_sample.shape
    L = w_in.shape[0]
    P = cache_fox_k.shape[2]
    TP, TS = B * S, NB * TN
    T = TP + TS
    NG = T // GROUP
    W = FOX_HEADS * FOX_DH
    KW = GLA_HEADS * GLA_DK
    tm = 512
    gla_nc = 4

    x3 = jnp.concatenate([x_prompt.reshape(TP, D), x_sample.reshape(TS, D)], axis=0).reshape(NG, GROUP, D)

    n_c = B + NB
    c_all = jnp.zeros((-(-n_c // 8) * 8, D), F32).at[:n_c].set(jnp.concatenate([c_prompt, c_sample], axis=0))
    mods = _ada(c_all, w_ada, b_ada)[:, :n_c].reshape(L, n_c, 6, D)

    segb = _segment_sum_matrix()
    place, one = _bias_placement()
    kc_all = jnp.transpose(cache_fox_k, (0, 1, 3, 4, 2)).reshape(L, NB, W, P)
    vc_all = jnp.transpose(cache_fox_v, (0, 1, 3, 4, 2)).reshape(L, NB, W, P)
    lfc_all = jnp.swapaxes(cache_fox_lf, 2, 3)
    s_prompt0 = jnp.zeros((B, KW, GLA_DV), F32)

    outs = {n: [] for n in ("kp", "vp", "lp", "sp", "kn", "vn", "ln", "sn")}
    for l in range(L):
        m = [(mods[l, :B, j][:, None, :], mods[l, B:, j][:, None, :]) for j in range(6)]
        w, wvt, wm, bm, wa, ba = _pack_in_weights(w_in[l], b_fox_f[l], w_gla_a2[l], b_gla_a[l])
        (fq, fk_p, fk_s, fv_p, fv_s, fkb, vt, qx, kx, gq, gk, gv, gr, la, lf) = _inproj(
            x3, m[0], m[1], g_attn[l].reshape(1, D), w, wvt, wm, bm, wa, ba, place, one,
            tm=tm, blocks_per_seq=S // tm, n_prompt=TP)
        lf_tok = lf[:, FF_LANE:FF_LANE + FOX_HEADS]

        fo_p = _fox_prompt(fq, qx, fkb, kx, vt, B=B, S=S, t=512)
        lfn3 = lf_tok[TP:].reshape(NB, TN, FOX_HEADS).transpose(0, 2, 1)
        fo_s = _fox_sample(fq[TP:].reshape(NB, TN, W), fkb[TP:].reshape(NB, TN, W),
                           fv_s.astype(BF16).reshape(NB, TN, W),
                           lfn3, kc_all, vc_all, lfc_all, l, tk=1024)

        go_p, s_p = _gla(gq, gk, gv, la, s_prompt0, segb, C=64, NC=gla_nc, chain=True, nseq=B,
                         nsteps=S // (64 * gla_nc), row_block_off=0)
        go_s, s_n = _gla(gq, gk, gv, la, state_gla[l].reshape(NB, KW, GLA_DV), segb, C=TN, NC=gla_nc, chain=False,
                         nseq=NB // gla_nc, nsteps=1, row_block_off=TP // (TN * gla_nc))

        x3 = _outproj(fo_p, fo_s, go_p, go_s, gr, x3, m[2], g_gla[l].reshape(1, GLA_DV), w_o[l].astype(BF16),
                      tm=tm, blocks_per_seq=S // tm)

        if l % 2 == 0:
            x3 = _ffn(x3, m[3], m[4], m[5], g_ffn[l].reshape(1, D), w_ffn_gate, w_ffn_up, w_ffn_down, l // 2,
                      tm=1024, tf=256, n_prompt=TP, seq=S)
        else:
            x3 = _moe(x3, m[3], m[4], m[5], g_ffn[l].reshape(1, D), w_router[l // 2], w_moe_gate, w_moe_up,
                      w_moe_down, l // 2, tm=tm, tmE=1056, tf=256, chunk=96, n_prompt=TP, seq=S)

        outs["kp"].append(fk_p.reshape(B, S, FOX_HEADS, FOX_DH))
        outs["vp"].append(fv_p.reshape(B, S, FOX_HEADS, FOX_DH))
        outs["lp"].append(lf_tok[:TP].reshape(B, S, FOX_HEADS))
        outs["sp"].append(s_p.reshape(B, GLA_HEADS, GLA_DK, GLA_DV))
        outs["kn"].append(fk_s.reshape(NB, TN, FOX_HEADS, FOX_DH))
        outs["vn"].append(fv_s.reshape(NB, TN, FOX_HEADS, FOX_DH))
        outs["ln"].append(lf_tok[TP:].reshape(NB, TN, FOX_HEADS))
        outs["sn"].append(s_n.reshape(NB, GLA_HEADS, GLA_DK, GLA_DV))

    y_p, y_s = _final_norm(x3.reshape(T, D), g_final.reshape(1, D), tm=tm, n_prompt=TP)
    st = lambda n: jnp.stack(outs[n])
    return (y_p.reshape(B, S, D), y_s.reshape(NB, TN, D),
            st("kp"), st("vp"), st("lp"), st("sp"), st("kn"), st("vn"), st("ln"), st("sn"))
```

```python
import functools
import math

import jax
import jax.numpy as jnp
from jax import lax
from jax.experimental import pallas as pl
from jax.experimental.pallas import tpu as pltpu

F32, BF16, I32 = jnp.float32, jnp.bfloat16, jnp.int32
EPS = 1e-6
NEG = -1e30
LOG2E = math.log2(math.e)
GROUP = 32
FOX_HEADS, FOX_DH = 8, 64
GLA_HEADS, GLA_DK, GLA_DV = 4, 64, 128
GLA_RANK = 16
GLA_TAU = 16.0
N_EXPERTS = 8
SUB = 32
FF_LANE = 16
VMEM_LIMIT = 56 * 1024 * 1024


def _cp(*sem):
    return pltpu.CompilerParams(dimension_semantics=sem, vmem_limit_bytes=VMEM_LIMIT)


def _dot(a, b):
    return jnp.dot(a, b, preferred_element_type=F32)


def _dot_nt(a, b):
    return lax.dot_general(a, b, (((1,), (1,)), ((), ())), preferred_element_type=F32)


def _dot_tn(a, b):
    return lax.dot_general(a, b, (((0,), (0,)), ((), ())), preferred_element_type=F32)


def _split3(x):
    h = x.astype(BF16)
    r = x - h.astype(F32)
    m = r.astype(BF16)
    l = (r - m.astype(F32)).astype(BF16)
    return h, m, l


def _dot3(x, u):
    h, m, l = _split3(x)
    return _dot(h, u) + _dot(m, u) + _dot(l, u)


def _dot3_l(u, x):
    h, m, l = _split3(x)
    return _dot(u, h) + _dot(u, m) + _dot(u, l)


def _dot3_tn(x, u):
    h, m, l = _split3(x)
    return _dot_tn(h, u) + _dot_tn(m, u) + _dot_tn(l, u)


def _silu(x):
    return x * (1.0 / (1.0 + jnp.exp(-x)))


def _log_sigmoid(x):
    return jnp.minimum(x, 0.0) - jnp.log1p(jnp.exp(-jnp.abs(x)))


def _norm_mod(x3, g, shift, scale):
    ms = jnp.mean(x3 * x3, axis=-1, keepdims=True)
    return (x3 * lax.rsqrt(ms + EPS) * g) * (1.0 + scale) + shift


def _mod(p_ref, s_ref, is_prompt):
    return jnp.where(is_prompt, p_ref[...], s_ref[...])


def _mod_specs(G, D, nbp, bps, grid_rank=1):
    pidx = lambda i: (jnp.minimum(i, nbp - 1) // bps, 0, 0)
    sidx = lambda i: (jnp.maximum(i - nbp, 0), 0, 0)
    if grid_rank == 1:
        return [pl.BlockSpec((1, 1, D), lambda i: pidx(i)), pl.BlockSpec((G, 1, D), lambda i: sidx(i))]
    return [pl.BlockSpec((1, 1, D), lambda i, j: pidx(i)), pl.BlockSpec((G, 1, D), lambda i, j: sidx(i))]


def _iota(shape, axis):
    return lax.broadcasted_iota(I32, shape, axis)


def _ones_where(cond):
    return jnp.where(cond, 1.0, 0.0).astype(BF16)


def _ada_kernel(c_ref, w_ref, b_ref, o_ref):
    a = _silu(c_ref[...]).astype(BF16)
    o_ref[0] = _dot(a, w_ref[0].astype(BF16)) + b_ref[0]


def _ada(c_all, w_ada, b_ada):
    L, D, N = w_ada.shape
    R = c_all.shape[0]
    tn = 1536
    return pl.pallas_call(
        _ada_kernel,
        grid=(L, N // tn),
        in_specs=[pl.BlockSpec((R, D), lambda l, j: (0, 0)),
                  pl.BlockSpec((1, D, tn), lambda l, j: (l, 0, j)),
                  pl.BlockSpec((1, 1, tn), lambda l, j: (l, 0, j))],
        out_specs=pl.BlockSpec((1, R, tn), lambda l, j: (l, 0, j)),
        out_shape=jax.ShapeDtypeStruct((L, R, N), F32),
        compiler_params=_cp("parallel", "parallel"),
    )(c_all, w_ada, b_ada.reshape(L, 1, N))


def _inproj_kernel(x_ref, shp_ref, shs_ref, scp_ref, scs_ref, g_ref, w_ref, wvt_ref, wm_ref, bm_ref, wa_ref, ba_ref,
                   place_ref, one_ref,
                   fq_ref, fkp_ref, fks_ref, fvp_ref, fvs_ref, fkb_ref, vt_ref, qx_ref, kx_ref, gq_ref, gk_ref, gv_ref,
                   gr_ref, la_ref, lf_ref, carry_ref, *, blocks_per_seq, nbp):
    i = pl.program_id(0)
    is_prompt = i < nbp
    G, R, D = x_ref.shape
    tm = G * R
    h = _norm_mod(x_ref[...], g_ref[...], _mod(shp_ref, shs_ref, is_prompt), _mod(scp_ref, scs_ref, is_prompt))
    h = h.reshape(tm, D).astype(BF16)

    fq_ref[...] = (_dot(h, w_ref[:, 0:512]) * (LOG2E * FOX_DH ** -0.5)).astype(BF16)
    zk = _dot(h, w_ref[:, 512:1024])
    fkb_ref[...] = zk.astype(BF16)
    zv = _dot(h, w_ref[:, 1024:1536])

    @pl.when(is_prompt)
    def _():
        fkp_ref[...] = zk
        fvp_ref[...] = zv

    @pl.when(jnp.logical_not(is_prompt))
    def _():
        fks_ref[...] = zk
        fvs_ref[...] = zv

    vt_ref[...] = _dot_nt(wvt_ref[...], h).astype(BF16)
    gq_ref[...] = _dot(h, w_ref[:, 1536:1792]) * (GLA_DK ** -0.5)
    gk_ref[...] = _dot(h, w_ref[:, 1792:2048])
    gv_ref[...] = _dot(h, w_ref[:, 2048:2560]).astype(BF16)
    gr_ref[...] = _dot(h, w_ref[:, 2560:3072])

    zm = _dot(h, wm_ref[...])
    la_pre = _dot(zm.astype(BF16), wa_ref[...]) + ba_ref[...]
    la_ref[...] = _log_sigmoid(la_pre) * (1.0 / GLA_TAU)
    lf = _log_sigmoid(zm + bm_ref[...])
    lf_ref[...] = lf

    @pl.when(i % blocks_per_seq == 0)
    def _():
        carry_ref[...] = jnp.zeros_like(carry_ref)

    half = tm // 2
    ltri = _ones_where(_iota((half, half), 1) <= _iota((half, half), 0))
    parts = jnp.concatenate(_split3(lf), axis=1)

    def fold(c):
        return c[:, 0:128] + c[:, 128:256] + c[:, 256:384]

    f0 = fold(_dot(ltri, parts[:half])) + carry_ref[0:1, :]
    f1 = fold(_dot(ltri, parts[half:])) + f0[half - 1:half, :]
    carry_ref[...] = jnp.broadcast_to(f1[half - 1:half, :], carry_ref.shape)
    f = jnp.concatenate([f0, f1], axis=0) * LOG2E

    xk = _dot(jnp.concatenate(_split3(f), axis=1), place_ref[...]) + one_ref[...]
    qx_ref[...] = xk[:, 0:128].astype(BF16)
    kx_ref[...] = xk[:, 128:256].astype(BF16)


def _inproj(x3, shift, scale, g, w, wvt, wm, bm, wa, ba, place, one, *, tm, blocks_per_seq, n_prompt):
    NG, R, D = x3.shape
    T = NG * R
    G = tm // R
    nb = T // tm
    nbp = n_prompt // tm
    row = lambda n: pl.BlockSpec((tm, n), lambda i: (i, 0))
    prow = lambda n: pl.BlockSpec((tm, n), lambda i: (jnp.minimum(i, nbp - 1), 0))
    srow = lambda n: pl.BlockSpec((tm, n), lambda i: (jnp.maximum(i - nbp, 0), 0))
    full = lambda a: pl.BlockSpec(a.shape, lambda i: (0,) * a.ndim)
    outs = [
        (jax.ShapeDtypeStruct((T, 512), BF16), row(512)),
        (jax.ShapeDtypeStruct((n_prompt, 512), F32), prow(512)),
        (jax.ShapeDtypeStruct((T - n_prompt, 512), F32), srow(512)),
        (jax.ShapeDtypeStruct((n_prompt, 512), F32), prow(512)),
        (jax.ShapeDtypeStruct((T - n_prompt, 512), F32), srow(512)),
        (jax.ShapeDtypeStruct((T, 512), BF16), row(512)),
        (jax.ShapeDtypeStruct((512, T), BF16), pl.BlockSpec((512, tm), lambda i: (0, i))),
        (jax.ShapeDtypeStruct((T, 128), BF16), row(128)),
        (jax.ShapeDtypeStruct((T, 128), BF16), row(128)),
        (jax.ShapeDtypeStruct((T, 256), F32), row(256)),
        (jax.ShapeDtypeStruct((T, 256), F32), row(256)),
        (jax.ShapeDtypeStruct((T, 512), BF16), row(512)),
        (jax.ShapeDtypeStruct((T, 512), F32), row(512)),
        (jax.ShapeDtypeStruct((T, 256), F32), row(256)),
        (jax.ShapeDtypeStruct((T, 128), F32), row(128)),
    ]
    return pl.pallas_call(
        functools.partial(_inproj_kernel, blocks_per_seq=blocks_per_seq, nbp=nbp),
        grid=(nb,),
        in_specs=[pl.BlockSpec((G, R, D), lambda i: (i, 0, 0))]
                 + _mod_specs(G, D, nbp, blocks_per_seq) + _mod_specs(G, D, nbp, blocks_per_seq)
                 + [full(g), full(w), full(wvt), full(wm), full(bm), full(wa), full(ba), full(place), full(one)],
        out_specs=[o[1] for o in outs],
        out_shape=[o[0] for o in outs],
        scratch_shapes=[pltpu.VMEM((8, 128), F32)],
        compiler_params=_cp("arbitrary"),
    )(x3, *shift, *scale, g, w, wvt, wm, bm, wa, ba, place, one)


def _fox_prompt_kernel(q_ref, qx_ref, k_ref, kx_ref, vt_ref, o_ref, sa_scr, sb_scr, m_scr, l_scr, acc_scr, *, t):
    p = pl.program_id(1)
    i = pl.program_id(2)
    lane = _iota((1, 128), 1)
    q2 = q_ref[...]
    qx = qx_ref[...]
    zero = jnp.zeros_like(q2)
    qa = []
    for hh in range(2):
        mine = (lane < FOX_DH) if hh == 0 else (lane >= FOX_DH)
        qa.append(jnp.concatenate([jnp.where(mine, q2, zero),
                                   jnp.where(lane // 16 == 2 * p + hh, qx, zero)], axis=1))
    m_scr[...] = jnp.full(m_scr.shape, NEG, F32)
    l_scr[...] = jnp.zeros_like(l_scr)
    acc_scr[...] = jnp.zeros_like(acc_scr)

    def key_start(kb):
        return pl.multiple_of(kb * t, t)

    def scores(kb):
        k0 = key_start(kb)
        ka = jnp.concatenate([k_ref[pl.ds(k0, t), :], kx_ref[pl.ds(k0, t), :]], axis=1)
        return [_dot_nt(ka, qa[hh]) for hh in range(2)]

    def scores_into(kb, buf, diagonal=False):
        st = scores(kb)
        if diagonal:
            visible = _iota((t, t), 0) <= _iota((t, t), 1)
            st = [jnp.where(visible, s_, NEG) for s_ in st]
        for hh in range(2):
            buf[hh] = st[hh]

    def consume(kb, st):
        k0 = key_start(kb)
        for hh in range(2):
            m_old = m_scr[hh]
            m_new = jnp.maximum(m_old, jnp.max(st[hh], axis=0, keepdims=True))
            alpha = jnp.exp2(m_old - m_new)
            pt = jnp.exp2(st[hh] - m_new)
            l_scr[hh] = alpha * l_scr[hh] + jnp.sum(pt, axis=0, keepdims=True)
            m_scr[hh] = m_new
            rows = slice(FOX_DH * hh, FOX_DH * (hh + 1))
            acc_scr[rows, :] = acc_scr[rows, :] * alpha + _dot(vt_ref[rows, pl.ds(k0, t)], pt.astype(BF16))

    def consume_from(kb, buf):
        consume(kb, [buf[hh] for hh in range(2)])

    @pl.when(i == 0)
    def _():
        scores_into(0, sa_scr, diagonal=True)
        consume_from(0, sa_scr)

    @pl.when(i >= 1)
    def _():
        scores_into(0, sa_scr)

    def pair(kk, c):
        kb = 2 * kk
        scores_into(kb + 1, sb_scr)
        consume_from(kb, sa_scr)
        scores_into(kb + 2, sa_scr)
        consume_from(kb + 1, sb_scr)
        return c

    lax.fori_loop(0, (i - 1) // 2, pair, 0)

    @pl.when(jnp.logical_and(i >= 1, i % 2 == 0))
    def _():
        scores_into(i - 1, sb_scr)
        consume_from(i - 2, sa_scr)
        scores_into(i, sa_scr, diagonal=True)
        consume_from(i - 1, sb_scr)
        consume_from(i, sa_scr)

    @pl.when(i % 2 == 1)
    def _():
        scores_into(i, sb_scr, diagonal=True)
        consume_from(i - 1, sa_scr)
        consume_from(i, sb_scr)

    inv =jnp.concatenate([jnp.broadcast_to(1.0 / l_scr[hh], (FOX_DH, t)) for hh in range(2)], axis=0)
    o_ref[...] = jnp.transpose(acc_scr[...] * inv).astype(o_ref.dtype)


def _fox_prompt(fq, qx, fkb, kx, vt, *, B, S, t):
    nq = S // t
    tq = t
    return pl.pallas_call(
        functools.partial(_fox_prompt_kernel, t=t),
        grid=(B, FOX_HEADS // 2, nq),
        in_specs=[pl.BlockSpec((tq, 128), lambda b, p, i: (b * nq + i, p)),
                  pl.BlockSpec((tq, 128), lambda b, p, i: (b * nq + i, 0)),
                  pl.BlockSpec((S, 128), lambda b, p, i: (b, p)),
                  pl.BlockSpec((S, 128), lambda b, p, i: (b, 0)),
                  pl.BlockSpec((128, S), lambda b, p, i: (p, b))],
        out_specs=pl.BlockSpec((tq, 128), lambda b, p, i: (b * nq + i, p)),
        out_shape=jax.ShapeDtypeStruct((B * S, 512), BF16),
        scratch_shapes=[pltpu.VMEM((2, t, t), F32), pltpu.VMEM((2, t, t), F32),
                        pltpu.VMEM((2, 1, t), F32), pltpu.VMEM((2, 1, t), F32),
                        pltpu.VMEM((128, t), F32)],
        compiler_params=_cp("parallel", "parallel", "arbitrary"),
    )(fq, qx, fkb, kx, vt)


def _suffix_sums(x, carry):
    n = x.shape[1]
    nch = n // 256
    xs = jnp.concatenate([x[:, c * 256:(c + 1) * 256] for c in range(nch)], axis=0)
    later = _ones_where(_iota((256, 256), 0) > _iota((256, 256), 1))
    r = _dot(jnp.concatenate(_split3(xs), axis=0), later)
    rows = 8 * nch
    w = r[0:rows] + r[rows:2 * rows] + r[2 * rows:3 * rows]
    tot = w[:, 0:1] + xs[:, 0:1]
    pieces = []
    run = carry
    for c in reversed(range(nch)):
        pieces.append(w[8 * c:8 * (c + 1)] + run)
        run = run + tot[8 * c:8 * (c + 1)]
    pieces.reverse()
    return jnp.concatenate(pieces, axis=1), run


def _fox_sample_kernel(q_ref, kn_ref, vn_ref, lfn_ref, kt_ref, vt_ref, lfc_ref, o_ref,
                       qbd_scr, fn_scr, m_scr, l_scr, acc_scr, g_scr, *, tk, nkb):
    kb = pl.program_id(1)
    TN = q_ref.shape[1]
    HT = FOX_HEADS * TN
    W = FOX_HEADS * FOX_DH

    def online(s, pv):
        m_old = m_scr[...]
        m_new = jnp.maximum(m_old, jnp.max(s, axis=1, keepdims=True))
        alpha = jnp.exp2(m_old - m_new)
        p = jnp.exp2(s - m_new)
        l_scr[...] = alpha * l_scr[...] + jnp.sum(p, axis=1, keepdims=True)
        m_scr[...] = m_new
        acc_scr[...] = acc_scr[...] * alpha + pv(p.astype(BF16))

    @pl.when(kb == 0)
    def _():
        q = q_ref[0]
        qt = jnp.concatenate([q] * FOX_HEADS, axis=0)
        same = (_iota((HT, W), 0) // TN) == (_iota((HT, W), 1) // FOX_DH)
        qbd = jnp.where(same, qt, jnp.zeros_like(qt))
        qbd_scr[...] = qbd
        tri = _ones_where(_iota((TN, TN), 0) <= _iota((TN, TN), 1))
        fnt = _dot3(lfn_ref[0] * LOG2E, tri)
        rep = jnp.broadcast_to(fnt[:, None, :], (FOX_HEADS, TN, TN)).reshape(HT, TN)
        tpos = _iota((HT, TN), 0) % TN
        jpos = _iota((HT, TN), 1)
        fn_col = jnp.sum(jnp.where(tpos == jpos, rep, 0.0), axis=1, keepdims=True)
        fn_scr[...] = fn_col
        m_scr[...] = jnp.full(m_scr.shape, NEG, F32)
        l_scr[...] = jnp.zeros_like(l_scr)
        acc_scr[...] = jnp.zeros_like(acc_scr)
        g_scr[...] = jnp.zeros_like(g_scr)
        s = _dot_nt(qbd, kn_ref[0]) + fn_col - rep
        online(jnp.where(jpos <= tpos, s, NEG), lambda p: _dot(p, vn_ref[0]))

    gl, total = _suffix_sums(lfc_ref[0, 0] * LOG2E, g_scr[:, 0:1])
    g_scr[...] = jnp.broadcast_to(total, g_scr.shape)
    grep = jnp.broadcast_to(gl[:, None, :], (FOX_HEADS, TN, tk)).reshape(HT, tk)
    s = _dot(qbd_scr[...], kt_ref[0, 0].astype(BF16)) + fn_scr[...] + grep
    online(s, lambda p: _dot_nt(p, vt_ref[0, 0].astype(BF16)))

    @pl.when(kb == nkb - 1)
    def _():
        accn = acc_scr[...] * (1.0 / l_scr[...])
        lane_head = _iota((TN, W), 1) // FOX_DH
        out = jnp.zeros((TN, W), F32)
        for h in range(FOX_HEADS):
            out = out + jnp.where(lane_head == h, accn[h * TN:(h + 1) * TN, :], 0.0)
        o_ref[...] = out.astype(o_ref.dtype)


def _fox_sample(q3, kn3, vn3, lfn3, kc_t, vc_t, lfc_t, layer, *, tk):
    NB, TN, W = q3.shape
    P = kc_t.shape[3]
    nkb = P // tk
    HT = FOX_HEADS * TN
    new = lambda: pl.BlockSpec((1, TN, W), lambda b, k: (b, 0, 0))
    return pl.pallas_call(
        functools.partial(_fox_sample_kernel, tk=tk, nkb=nkb),
        grid=(NB, nkb),
        in_specs=[new(), new(), new(),
                  pl.BlockSpec((1, FOX_HEADS, TN), lambda b, k: (b, 0, 0)),
                  pl.BlockSpec((1, 1, W, tk), lambda b, k: (layer, b, 0, nkb - 1 - k)),
                  pl.BlockSpec((1, 1, W, tk), lambda b, k: (layer, b, 0, nkb - 1 - k)),
                  pl.BlockSpec((1, 1, FOX_HEADS, tk), lambda b, k: (layer, b, 0, nkb - 1 - k))],
        out_specs=pl.BlockSpec((TN, W), lambda b, k: (b, 0)),
        out_shape=jax.ShapeDtypeStruct((NB * TN, W), BF16),
        scratch_shapes=[pltpu.VMEM((HT, W), BF16), pltpu.VMEM((HT, 1), F32), pltpu.VMEM((HT, 1), F32),
                        pltpu.VMEM((HT, 1), F32), pltpu.VMEM((HT, W), F32), pltpu.VMEM((FOX_HEADS, 128), F32)],
        compiler_params=_cp("parallel", "arbitrary"),
    )(q3, kn3, vn3, lfn3, kc_t, vc_t, lfc_t)


def _gla_intra(q, k, v, la, segb, tri, ones_cv):
    C = q.shape[0]
    sub = min(SUB, C)
    KW, VW = GLA_HEADS * GLA_DK, GLA_HEADS * GLA_DV
    b = _dot3_l(tri, la)
    blast = b[C - 1:C, :]
    qe = (q * jnp.exp(b)).astype(BF16)
    kd = (k * jnp.exp(blast - b)).astype(BF16)
    blocks = []
    for blk in range(C // sub):
        r0 = blk * sub
        bi, qi = b[r0:r0 + sub], q[r0:r0 + sub]
        o_blk = None
        for i0 in range(0, sub, 8):
            n_i = sub - i0
            bj, kj = b[r0 + i0:r0 + i0 + 8], k[r0 + i0:r0 + i0 + 8]
            vj = v[r0 + i0:r0 + i0 + 8].astype(F32)
            diff = bi[i0:][None, :, :] - bj[:, None, :]
            causal = _iota((8, n_i, 1), 1) >= _iota((8, n_i, 1), 0)
            tm3 = jnp.exp(jnp.where(causal, diff, -jnp.inf)) * qi[i0:][None, :, :] * kj[:, None, :]
            arep = _dot(tm3.reshape(8 * n_i, KW).astype(BF16), segb)
            og = jnp.sum(arep.reshape(8, n_i, VW) * vj[:, None, :], axis=0)
            if i0:
                og = jnp.concatenate([jnp.zeros((i0, VW), F32), og], axis=0)
            o_blk = og if o_blk is None else o_blk + og
        if blk > 0:
            bref = b[r0 - 1:r0]
            qs = (qi * jnp.exp(bi - bref)).astype(BF16)
            ks = (k[:r0] * jnp.exp(bref - b[:r0])).astype(BF16)
            parts = []
            for h in range(GLA_HEADS):
                ks_, vs_ = slice(h * GLA_DK, (h + 1) * GLA_DK), slice(h * GLA_DV, (h + 1) * GLA_DV)
                a = _dot_nt(qs[:, ks_], ks[:, ks_])
                parts.append(_dot(a.astype(BF16), v[:r0, vs_]))
            o_blk = o_blk + jnp.concatenate(parts, axis=1)
        blocks.append(o_blk)
    o_intra = jnp.concatenate(blocks, axis=0)
    dec = jnp.exp(_dot3_tn(la, ones_cv))
    upd = _dot_tn(kd, v)
    upd = jnp.concatenate([upd[h * GLA_DK:(h + 1) * GLA_DK, h * GLA_DV:(h + 1) * GLA_DV]
                           for h in range(GLA_HEADS)], axis=0)
    return qe, o_intra, dec, upd


def _gla_kernel(q_ref, k_ref, v_ref, la_ref, s0_ref, segb_ref, o_ref, sfin_ref, *scr, C, NC, chain, nsteps):
    if chain:
        (s_scr,) = scr

        @pl.when(pl.program_id(1) == 0)
        def _():
            s_scr[...] = s0_ref[0]

    tri = _ones_where(_iota((C, C), 1) <= _iota((C, C), 0))
    ones_cv = jnp.ones((C, GLA_DV), BF16)
    segb = segb_ref[...]
    units = []
    for u in range(NC):
        rows = slice(u * C, (u + 1) * C)
        units.append(_gla_intra(q_ref[rows, :], k_ref[rows, :], v_ref[rows, :], la_ref[rows, :], segb, tri, ones_cv))

    state = s_scr[...] if chain else None
    for u, (qe, o_intra, dec, upd) in enumerate(units):
        if not chain:
            state = s0_ref[u]
        s_bf = state.astype(BF16)
        o_inter = jnp.concatenate(
            [_dot(qe[:, h * GLA_DK:(h + 1) * GLA_DK], s_bf[h * GLA_DK:(h + 1) * GLA_DK, :]) for h in range(GLA_HEADS)],
            axis=1)
        o_ref[u * C:(u + 1) * C, :] = o_intra + o_inter
        state = dec * state + upd
        if not chain:
            sfin_ref[u] = state
    if chain:
        s_scr[...] = state

        @pl.when(pl.program_id(1) == nsteps - 1)
        def _():
            sfin_ref[0] = state


def _gla(gq, gk, gv, la, s0, segb, *, C, NC, chain, nseq, nsteps, row_block_off):
    KW, VW = GLA_HEADS * GLA_DK, GLA_HEADS * GLA_DV
    rows = NC * C
    ns = 1 if chain else NC
    tok = lambda n: pl.BlockSpec((rows, n), lambda s, c: (row_block_off + s * nsteps + c, 0))
    return pl.pallas_call(
        functools.partial(_gla_kernel, C=C, NC=NC, chain=chain, nsteps=nsteps),
        grid=(nseq, nsteps),
        in_specs=[tok(KW), tok(KW), tok(VW), tok(KW),
                  pl.BlockSpec((ns, KW, GLA_DV), lambda s, c: (s, 0, 0)),
                  pl.BlockSpec(segb.shape, lambda s, c: (0, 0))],
        out_specs=[pl.BlockSpec((rows, VW), lambda s, c: (s * nsteps + c, 0)),
                   pl.BlockSpec((ns, KW, GLA_DV), lambda s, c: (s, 0, 0))],
        out_shape=[jax.ShapeDtypeStruct((nseq * nsteps * rows, VW), F32),
                   jax.ShapeDtypeStruct((nseq * ns, KW, GLA_DV), F32)],
        scratch_shapes=[pltpu.VMEM((KW, GLA_DV), F32)] if chain else [],
        compiler_params=_cp("parallel", "arbitrary"),
    )(gq, gk, gv, la, s0, segb)


def _outproj_kernel(fop_ref, fos_ref, gop_ref, gos_ref, gr_ref, x_ref, gtp_ref, gts_ref, gg_ref, wo_ref, o_ref, *, nbp):
    G, R, D = x_ref.shape
    is_prompt = pl.program_id(0) < nbp
    fo = jnp.where(is_prompt, fop_ref[...], fos_ref[...])
    go = jnp.where(is_prompt, gop_ref[...], gos_ref[...])
    parts = []
    for h in range(GLA_HEADS):
        seg = go[:, h * GLA_DV:(h + 1) * GLA_DV]
        ms = jnp.mean(seg * seg, axis=-1, keepdims=True)
        parts.append(seg * lax.rsqrt(ms + EPS) * gg_ref[...])
    gn = jnp.concatenate(parts, axis=1) * _silu(gr_ref[...])
    mix = _dot(fo, wo_ref[0:512, :]) + _dot(gn.astype(BF16), wo_ref[512:1024, :])
    o_ref[...] = x_ref[...] + _mod(gtp_ref, gts_ref, is_prompt) * mix.reshape(G, R, D)


def _outproj(fo_p, fo_s, go_p, go_s, gr, x3, gate, gg, wo, *, tm, blocks_per_seq):
    NG, R, D = x3.shape
    G = tm // R
    nbp = fo_p.shape[0] // tm
    row = lambda n: pl.BlockSpec((tm, n), lambda i: (i, 0))
    prow = lambda n: pl.BlockSpec((tm, n), lambda i: (jnp.minimum(i, nbp - 1), 0))
    srow = lambda n: pl.BlockSpec((tm, n), lambda i: (jnp.maximum(i - nbp, 0), 0))
    return pl.pallas_call(
        functools.partial(_outproj_kernel, nbp=nbp),
        grid=(NG // G,),
        in_specs=[prow(512), srow(512), prow(512), srow(512), row(512),
                  pl.BlockSpec((G, R, D), lambda i: (i, 0, 0))]
                 + _mod_specs(G, D, nbp, blocks_per_seq)
                 + [pl.BlockSpec(gg.shape, lambda i: (0, 0)),
                    pl.BlockSpec(wo.shape, lambda i: (0, 0))],
        out_specs=pl.BlockSpec((G, R, D), lambda i: (i, 0, 0)),
        out_shape=jax.ShapeDtypeStruct(x3.shape, F32),
        compiler_params=_cp("parallel"),
    )(fo_p, fo_s, go_p, go_s, gr, x3, *gate, gg, wo)


def _ffn_kernel(x_ref, shp_ref, shs_ref, scp_ref, scs_ref, gtp_ref, gts_ref, g_ref, wg_ref, wu_ref, wd_ref, o_ref,
                h_scr, acc_scr, *, nf, nbp):
    j = pl.program_id(1)
    is_prompt = pl.program_id(0) < nbp
    G, R, D = x_ref.shape

    @pl.when(j == 0)
    def _():
        h = _norm_mod(x_ref[...], g_ref[...], _mod(shp_ref, shs_ref, is_prompt), _mod(scp_ref, scs_ref, is_prompt))
        h_scr[...] = h.reshape(G * R, D).astype(BF16)
        acc_scr[...] = jnp.zeros_like(acc_scr)

    h = h_scr[...]
    a = _dot(h, wg_ref[0].astype(BF16))
    u = _dot(h, wu_ref[0].astype(BF16))
    acc_scr[...] += _dot((_silu(a) * u).astype(BF16), wd_ref[0].astype(BF16))

    @pl.when(j == nf - 1)
    def _():
        o_ref[...] = x_ref[...] + _mod(gtp_ref, gts_ref, is_prompt) * acc_scr[...].reshape(G, R, D)


def _ffn(x3, shift, scale, gate, g, wg, wu, wd, li, *, tm, tf, n_prompt, seq):
    NG, R, D = x3.shape
    G = tm // R
    F = wg.shape[-1]
    nf = F // tf
    nbp = n_prompt // tm
    grp = lambda n: pl.BlockSpec((G, n, D), lambda i, j: (i, 0, 0))
    mods = lambda: _mod_specs(G, D, nbp, seq // tm, grid_rank=2)
    return pl.pallas_call(
        functools.partial(_ffn_kernel, nf=nf, nbp=nbp),
        grid=(NG // G, nf),
        in_specs=[grp(R)] + mods() + mods() + mods()
                 + [pl.BlockSpec(g.shape, lambda i, j: (0, 0)),
                    pl.BlockSpec((1, D, tf), lambda i, j: (li, 0, j)),
                    pl.BlockSpec((1, D, tf), lambda i, j: (li, 0, j)),
                    pl.BlockSpec((1, tf, D), lambda i, j: (li, j, 0))],
        out_specs=grp(R),
        out_shape=jax.ShapeDtypeStruct(x3.shape, F32),
        scratch_shapes=[pltpu.VMEM((tm, D), BF16), pltpu.VMEM((tm, D), F32)],
        compiler_params=_cp("parallel", "arbitrary"),
    )(x3, *shift, *scale, *gate, g, wg, wu, wd)


def _router_kernel(x_ref, shp_ref, shs_ref, scp_ref, scs_ref, g_ref, wr_ref, h_ref, idx_ref, gw_ref, *, nbp):
    G, R, D = x_ref.shape
    is_prompt = pl.program_id(0) < nbp
    h = _norm_mod(x_ref[...], g_ref[...], _mod(shp_ref, shs_ref, is_prompt), _mod(scp_ref, scs_ref, is_prompt))
    h = h.reshape(G * R, D)
    h_ref[...] = h
    hh, hm, hl = _split3(h)
    wh, wm, wl = wr_ref[0], wr_ref[1], wr_ref[2]
    logits = (_dot(hh, wh) + _dot(hh, wm) + _dot(hm, wh)
              + _dot(hh, wl) + _dot(hm, wm) + _dot(hl, wh))
    lane = _iota(logits.shape, 1)
    logits = jnp.where(lane < N_EXPERTS, logits, -jnp.inf)
    v1 = jnp.max(logits, axis=1, keepdims=True)
    i1 = jnp.min(jnp.where(logits == v1, lane, 128), axis=1, keepdims=True)
    rest = jnp.where(lane == i1, -jnp.inf, logits)
    v2 = jnp.max(rest, axis=1, keepdims=True)
    i2 = jnp.min(jnp.where(rest == v2, lane, 128), axis=1, keepdims=True)
    e = jnp.exp(v2 - v1)
    g1 = 1.0 / (1.0 + e)
    idx_ref[...] = jnp.where(lane == 0, i1, jnp.where(lane == 1, i2, 0))
    gw_ref[...] = jnp.where(lane == 0, g1, jnp.where(lane == 1, e * g1, 0.0))


def _router(x3, shift, scale, g, wr3, *, tm, n_prompt, seq):
    NG, R, D = x3.shape
    T = NG * R
    G = tm // R
    nbp = n_prompt // tm
    return pl.pallas_call(
        functools.partial(_router_kernel, nbp=nbp),
        grid=(NG // G,),
        in_specs=[pl.BlockSpec((G, R, D), lambda i: (i, 0, 0))]
                 + _mod_specs(G, D, nbp, seq // tm) + _mod_specs(G, D, nbp, seq // tm)
                 + [pl.BlockSpec(g.shape, lambda i: (0, 0)),
                    pl.BlockSpec(wr3.shape, lambda i: (0, 0, 0))],
        out_specs=[pl.BlockSpec((tm, D), lambda i: (i, 0)),
                   pl.BlockSpec((tm, 128), lambda i: (i, 0)),
                   pl.BlockSpec((tm, 128), lambda i: (i, 0))],
        out_shape=[jax.ShapeDtypeStruct((T, D), F32), jax.ShapeDtypeStruct((T, 128), I32),
                   jax.ShapeDtypeStruct((T, 128), F32)],
        compiler_params=_cp("parallel"),
    )(x3, *shift, *scale, g, wr3)


def _row_copy(src_ref, src_row, dst_ref, dst_row, sem):
    return pltpu.make_async_copy(src_ref.at[pl.ds(src_row, 1), :], dst_ref.at[pl.ds(dst_row, 1), :], sem)


def _moe_ffn_kernel(te_ref, na_ref, src_ref, dst_ref, h_hbm, wg_ref, wu_ref, wd_ref, y_hbm, xbuf, h_scr, acc, gsem, ssem,
                    *, nf, tmE, chunk):
    del te_ref
    i, j = pl.program_id(0), pl.program_id(1)
    na = na_ref[0]
    slot = i % 2
    other = 1 - slot
    r0 = j * chunk

    def send_rows():
        base = i * tmE + r0
        for u in range(chunk):
            _row_copy(acc.at[other], r0 + u, y_hbm, dst_ref[base + u], ssem.at[slot]).start(priority=u % 2)

    @pl.when(jnp.logical_and(i == 0, j == 0))
    def _():
        def issue(r, c):
            _row_copy(h_hbm, src_ref[r], xbuf.at[0], r, gsem.at[0]).start()
            return c

        lax.fori_loop(0, tmE, issue, 0)
        acc[1] = jnp.zeros((tmE, acc.shape[2]), F32)

    @pl.when(jnp.logical_and(i <= na, j == 0))
    def _():
        pltpu.make_async_copy(h_hbm.at[pl.ds(0, tmE), :], xbuf.at[slot], gsem.at[slot]).wait()

        @pl.when(i >= 1)
        def _():
            pltpu.make_async_copy(acc.at[slot], y_hbm.at[pl.ds(0, tmE), :], ssem.at[other]).wait()

        h_scr[...] = xbuf[slot].astype(BF16)
        acc[slot] = jnp.zeros((tmE, acc.shape[2]), F32)

    @pl.when(i < na)
    def _():
        base = (i + 1) * tmE + r0
        for u in range(chunk):
            _row_copy(h_hbm, src_ref[base + u], xbuf.at[other], r0 + u, gsem.at[other]).start()
        send_rows()
        h = h_scr[...]
        a = _dot(h, wg_ref[0, 0].astype(BF16))
        u = _dot(h, wu_ref[0, 0].astype(BF16))
        acc[slot] += _dot((_silu(a) * u).astype(BF16), wd_ref[0, 0].astype(BF16))

    @pl.when(i == na)
    def _():
        send_rows()

        @pl.when(j == nf - 1)
        def _():
            pltpu.make_async_copy(acc.at[other], y_hbm.at[pl.ds(0, tmE), :], ssem.at[slot]).wait()


def _moe_ffn(tile_expert, n_active, src, dst, h, wg, wu, wd, li, *, n_tiles, tmE, tf, chunk):
    T, D = h.shape
    F = wg.shape[-1]
    nf = F // tf
    assert chunk * nf == tmE
    jj = lambda i, j, na: jnp.where(i < na[0], j, nf - 1)
    w_in = lambda: pl.BlockSpec((1, 1, D, tf), lambda i, j, te, na, sr, ds: (li, te[i], 0, jj(i, j, na)))
    return pl.pallas_call(
        functools.partial(_moe_ffn_kernel, nf=nf, tmE=tmE, chunk=chunk),
        grid_spec=pltpu.PrefetchScalarGridSpec(
            num_scalar_prefetch=4,
            grid=(n_tiles, nf),
            in_specs=[pl.BlockSpec(memory_space=pl.ANY), w_in(), w_in(),
                      pl.BlockSpec((1, 1, tf, D), lambda i, j, te, na, sr, ds: (li, te[i], jj(i, j, na), 0))],
            out_specs=pl.BlockSpec(memory_space=pl.ANY),
            scratch_shapes=[pltpu.VMEM((2, tmE, D), F32), pltpu.VMEM((tmE, D), BF16), pltpu.VMEM((2, tmE, D), F32),
                            pltpu.SemaphoreType.DMA((2,)), pltpu.SemaphoreType.DMA((2,))]),
        out_shape=jax.ShapeDtypeStruct((2 * T + tmE, D), F32),
        compiler_params=_cp("arbitrary", "arbitrary"),
    )(tile_expert, n_active, src, dst, h, wg, wu, wd)


def _combine_kernel(y0_ref, y1_ref, x_ref, gtp_ref, gts_ref, gw_ref, o_ref, *, nbp):
    G, R, D = x_ref.shape
    gw = gw_ref[...]
    y = gw[:, 0:1] * y0_ref[...] + gw[:, 1:2] * y1_ref[...]
    o_ref[...] = x_ref[...] + _mod(gtp_ref, gts_ref, pl.program_id(0) < nbp) * y.reshape(G, R, D)


def _combine(y2, x3, gate, gw, *, tm, n_prompt, seq):
    NG, R, D = x3.shape
    G = tm // R
    nb = NG // G
    nbp = n_prompt // tm
    return pl.pallas_call(
        functools.partial(_combine_kernel, nbp=nbp),
        grid=(nb,),
        in_specs=[pl.BlockSpec((tm, D), lambda i: (i, 0)),
                  pl.BlockSpec((tm, D), lambda i: (nb + i, 0)),
                  pl.BlockSpec((G, R, D), lambda i: (i, 0, 0))]
                 + _mod_specs(G, D, nbp, seq // tm)
                 + [pl.BlockSpec((tm, 128), lambda i: (i, 0))],
        out_specs=pl.BlockSpec((G, R, D), lambda i: (i, 0, 0)),
        out_shape=jax.ShapeDtypeStruct(x3.shape, F32),
        compiler_params=_cp("parallel"),
    )(y2, y2, x3, *gate, gw)


def _moe_plan(idx2, *, tmE, n_tiles):
    T = idx2.shape[0]
    R = n_tiles * tmE
    flat = idx2.reshape(-1)
    onehot = (flat[:, None] == jnp.arange(N_EXPERTS, dtype=I32)[None, :]).astype(I32)
    csum = jnp.cumsum(onehot, axis=0)
    rank = jnp.sum(csum * onehot, axis=1) - 1
    counts = csum[-1]
    padded = ((counts + tmE - 1) // tmE) * tmE
    ends = jnp.cumsum(padded)
    starts = ends - padded
    pos = jnp.sum(starts[None, :] * onehot, axis=1) + rank
    entry = jnp.full((R,), -1, I32).at[pos].set(jnp.arange(2 * T, dtype=I32))
    real = entry >= 0
    spare = 2 * T + jnp.arange(R, dtype=I32) % tmE
    src = jnp.where(real, entry // 2, 0)
    dst = jnp.where(real, (entry % 2) * T + entry // 2, spare)
    dst = jnp.concatenate([spare[:tmE], dst])
    n_active = ends[-1] // tmE
    tile_start = jnp.arange(n_tiles, dtype=I32) * tmE
    te = jnp.sum((tile_start[:, None] >= ends[None, :]).astype(I32), axis=1)
    last = jnp.sum(((n_active - 1) * tmE >= ends).astype(I32))
    te = jnp.where(jnp.arange(n_tiles) < n_active, te, last).astype(I32)
    return te, n_active.astype(I32).reshape(1), src.astype(I32), dst.astype(I32)


def _moe(x3, shift, scale, gate, g, wr, wg, wu, wd, li, *, tm, tmE, tf, chunk, n_prompt, seq):
    NG, R, D = x3.shape
    T = NG * R
    wr_pad = jnp.zeros((D, 128), F32).at[:, :N_EXPERTS].set(wr)
    wr3 = jnp.stack(_split3(wr_pad))
    h, idx, gw = _router(x3, shift, scale, g, wr3, tm=tm, n_prompt=n_prompt, seq=seq)
    n_tiles = -(-(2 * T) // tmE) + N_EXPERTS
    te, n_active, src, dst = _moe_plan(idx[:, :2], tmE=tmE, n_tiles=n_tiles)
    y2 = _moe_ffn(te, n_active, src, dst, h, wg, wu, wd, li, n_tiles=n_tiles, tmE=tmE, tf=tf, chunk=chunk)
    return _combine(y2, x3, gate, gw, tm=tm, n_prompt=n_prompt, seq=seq)


def _final_kernel(x_ref, g_ref, op_ref, os_ref, *, nbp):
    x = x_ref[...]
    ms = jnp.mean(x * x, axis=-1, keepdims=True)
    y = x * lax.rsqrt(ms + EPS) * g_ref[...]
    is_prompt = pl.program_id(0) < nbp

    @pl.when(is_prompt)
    def _():
        op_ref[...] = y

    @pl.when(jnp.logical_not(is_prompt))
    def _():
        os_ref[...] = y


def _final_norm(x2, g, *, tm, n_prompt):
    T, D = x2.shape
    nbp = n_prompt // tm
    return pl.pallas_call(
        functools.partial(_final_kernel, nbp=nbp),
        grid=(T // tm,),
        in_specs=[pl.BlockSpec((tm, D), lambda i: (i, 0)), pl.BlockSpec(g.shape, lambda i: (0, 0))],
        out_specs=[pl.BlockSpec((tm, D), lambda i: (jnp.minimum(i, nbp - 1), 0)),
                   pl.BlockSpec((tm, D), lambda i: (jnp.maximum(i - nbp, 0), 0))],
        out_shape=[jax.ShapeDtypeStruct((n_prompt, D), F32), jax.ShapeDtypeStruct((T - n_prompt, D), F32)],
        compiler_params=_cp("arbitrary"),
    )(x2, g)


def _segment_sum_matrix():
    r = jnp.arange(GLA_HEADS * GLA_DK)[:, None] // GLA_DK
    c = jnp.arange(GLA_HEADS * GLA_DV)[None, :] // GLA_DV
    return (r == c).astype(BF16)


def _bias_placement():
    place = jnp.zeros((384, 256), F32)
    one = jnp.zeros((1, 256), F32)
    for h in range(FOX_HEADS):
        for part in range(3):
            place = place.at[128 * part + FF_LANE + h, 16 * h + part].set(1.0)
            place = place.at[128 * part + FF_LANE + h, 128 + 16 * h + 3 + part].set(-1.0)
            one = one.at[0, 16 * h + 3 + part].set(1.0)
            one = one.at[0, 128 + 16 * h + part].set(1.0)
    return place.astype(BF16), one


def _pack_in_weights(w_in_l, b_f_l, w_a2_l, b_a_l):
    D = w_in_l.shape[0]
    o_ff, o_g = 1536, 1544
    o_ga = o_g + 256 + 256 + 512 + 512
    w = jnp.concatenate([w_in_l[:, :o_ff], w_in_l[:, o_g:o_ga]], axis=1).astype(BF16)
    wvt = w_in_l[:, 1024:1536].T.astype(BF16)
    wm = jnp.zeros((D, 128), F32).at[:, :GLA_RANK].set(w_in_l[:, o_ga:o_ga + GLA_RANK])
    wm = wm.at[:, FF_LANE:FF_LANE + FOX_HEADS].set(w_in_l[:, o_ff:o_g]).astype(BF16)
    bm = jnp.zeros((1, 128), F32).at[0, FF_LANE:FF_LANE + FOX_HEADS].set(b_f_l)
    wa = jnp.zeros((128, 256), F32).at[:GLA_RANK].set(w_a2_l).astype(BF16)
    return w, wvt, wm, bm, wa, b_a_l.reshape(1, -1)


def kernel(x_prompt, x_sample, cache_fox_k, cache_fox_v, cache_fox_lf, state_gla, c_prompt, c_sample, w_ada, b_ada, g_attn, g_ffn, w_in, b_fox_f, w_gla_a2, b_gla_a, g_gla, w_o, w_ffn_gate, w_ffn_up, w_ffn_down, w_router, w_moe_gate, w_moe_up, w_moe_down, g_final):
    B, S, D = x_prompt.shape
    NB, TN, _ = x_sample.shape
    L = w_in.shape[0]
    P = cache_fox_k.shape[2]
    TP, TS = B * S, NB * TN
    T = TP + TS
    NG = T // GROUP
    W = FOX_HEADS * FOX_DH
    KW = GLA_HEADS * GLA_DK
    tm = 512
    gla_nc = 8

    x3 = jnp.concatenate([x_prompt.reshape(TP, D), x_sample.reshape(TS, D)], axis=0).reshape(NG, GROUP, D)

    n_c = B + NB
    c_all = jnp.zeros((-(-n_c // 8) * 8, D), F32).at[:n_c].set(jnp.concatenate([c_prompt, c_sample], axis=0))
    mods = _ada(c_all, w_ada, b_ada)[:, :n_c].reshape(L, n_c, 6, D)

    segb = _segment_sum_matrix()
    place, one = _bias_placement()
    kc_all = jnp.transpose(cache_fox_k, (0, 1, 3, 4, 2)).reshape(L, NB, W, P)
    vc_all = jnp.transpose(cache_fox_v, (0, 1, 3, 4, 2)).reshape(L, NB, W, P)
    lfc_all = jnp.swapaxes(cache_fox_lf, 2, 3)
    s_prompt0 = jnp.zeros((B, KW, GLA_DV), F32)

    outs = {n: [] for n in ("kp", "vp", "lp", "sp", "kn", "vn", "ln", "sn")}
    for l in range(L):
        m = [(mods[l, :B, j][:, None, :], mods[l, B:, j][:, None, :]) for j in range(6)]
        w, wvt, wm, bm, wa, ba = _pack_in_weights(w_in[l], b_fox_f[l], w_gla_a2[l], b_gla_a[l])
        (fq, fk_p, fk_s, fv_p, fv_s, fkb, vt, qx, kx, gq, gk, gv, gr, la, lf) = _inproj(
            x3, m[0], m[1], g_attn[l].reshape(1, D), w, wvt, wm, bm, wa, ba, place, one,
            tm=tm, blocks_per_seq=S // tm, n_prompt=TP)
        lf_tok = lf[:, FF_LANE:FF_LANE + FOX_HEADS]

        fo_p = _fox_prompt(fq, qx, fkb, kx, vt, B=B, S=S, t=512)
        lfn3 = lf_tok[TP:].reshape(NB, TN, FOX_HEADS).transpose(0, 2, 1)
        fo_s = _fox_sample(fq[TP:].reshape(NB, TN, W), fkb[TP:].reshape(NB, TN, W),
                           fv_s.astype(BF16).reshape(NB, TN, W),
                           lfn3, kc_all, vc_all, lfc_all, l, tk=1024)

        go_p, s_p = _gla(gq, gk, gv, la, s_prompt0, segb, C=64, NC=gla_nc, chain=True, nseq=B,
                         nsteps=S // (64 * gla_nc), row_block_off=0)
        go_s, s_n = _gla(gq, gk, gv, la, state_gla[l].reshape(NB, KW, GLA_DV), segb, C=TN, NC=gla_nc, chain=False,
                         nseq=NB // gla_nc, nsteps=1, row_block_off=TP // (TN * gla_nc))

        x3 = _outproj(fo_p, fo_s, go_p, go_s, gr, x3, m[2], g_gla[l].reshape(1, GLA_DV), w_o[l].astype(BF16),
                      tm=tm, blocks_per_seq=S // tm)

        if l % 2 == 0:
            x3 = _ffn(x3, m[3], m[4], m[5], g_ffn[l].reshape(1, D), w_ffn_gate, w_ffn_up, w_ffn_down, l // 2,
                      tm=1024, tf=256, n_prompt=TP, seq=S)
        else:
            x3 = _moe(x3, m[3], m[4], m[5], g_ffn[l].reshape(1, D), w_router[l // 2], w_moe_gate, w_moe_up,
                      w_moe_down, l // 2, tm=tm, tmE=1056, tf=256, chunk=96, n_prompt=TP, seq=S)

        outs["kp"].append(fk_p.reshape(B, S, FOX_HEADS, FOX_DH))
        outs["vp"].append(fv_p.reshape(B, S, FOX_HEADS, FOX_DH))
        outs["lp"].append(lf_tok[:TP].reshape(B, S, FOX_HEADS))
        outs["sp"].append(s_p.reshape(B, GLA_HEADS, GLA_DK, GLA_DV))
        outs["kn"].append(fk_s.reshape(NB, TN, FOX_HEADS, FOX_DH))
        outs["vn"].append(fv_s.reshape(NB, TN, FOX_HEADS, FOX_DH))
        outs["ln"].append(lf_tok[TP:].reshape(NB, TN, FOX_HEADS))
        outs["sn"].append(s_n.reshape(NB, GLA_HEADS, GLA_DK, GLA_DV))

    y_p, y_s = _final_norm(x3.reshape(T, D), g_final.reshape(1, D), tm=tm, n_prompt=TP)
    st = lambda n: jnp.stack(outs[n])
    return (y_p.reshape(B, S, D), y_s.reshape(NB, TN, D),
            st("kp"), st("vp"), st("lp"), st("sp"), st("kn"), st("vn"), st("ln"), st("sn"))
```
